```python
import jax, jax.numpy as jnp
from jax import lax
import numpy as np

D_MODEL = 2048
BATCH = 8
SEQ = 2048
DEPTH = 4
DEC_BATCH = 8
DEC_SEQ = 16
PAST_LEN = 4096

CHUNK = 64
N_BRANCH = 4
MIX_W = 512
RWKV_HEADS = 8
RWKV_HEAD = 64
W_RANK = 96
A_RANK = 96
G_RANK = 256
GN_EPS = 64e-5
RWKV_COLS = 3 * MIX_W + W_RANK + A_RANK + G_RANK
RWKV_SPLITS = (MIX_W, 2 * MIX_W, 3 * MIX_W, 3 * MIX_W + W_RANK, 3 * MIX_W + W_RANK + A_RANK)
GMLP_CHUNK = 128
GMLP_GROUPS = 8
GMLP_GDIM = MIX_W // GMLP_GROUPS
CONV_W = 31
ATTN_HEADS = 8
ATTN_HEAD = 64
BAND_PAST_CHUNKS = 8
BAND_PAST = BAND_PAST_CHUNKS * CHUNK
REL_CLIP = 128
D_FF = ((8 * D_MODEL // 3 + 255) // 256) * 256
OFF_RWKV = 0
OFF_GMLP = OFF_RWKV + RWKV_COLS
OFF_CONV = OFF_GMLP + 2 * MIX_W
OFF_ATTN = OFF_CONV + 2 * MIX_W
OFF_GATE = OFF_ATTN + 3 * MIX_W
IN_COLS = OFF_GATE + N_BRANCH * D_MODEL

kernel_name = "hybrid_streaming_encoder_step"


def rms_norm(x, g, eps=1e-6):
    xf = x.astype(jnp.float32)
    y = xf * lax.rsqrt(jnp.mean(xf * xf, -1, keepdims=True) + eps)
    return (y * g.astype(jnp.float32)).astype(x.dtype)


def layer_norm(x, g, b, eps=1e-5):
    xf = x.astype(jnp.float32)
    mu = jnp.mean(xf, -1, keepdims=True)
    var = jnp.mean(jnp.square(xf - mu), -1, keepdims=True)
    y = (xf - mu) * lax.rsqrt(var + eps) * g.astype(jnp.float32) + b.astype(jnp.float32)
    return y.astype(x.dtype)


def wkv7_scan(S0, r, w, k, v, a, b):
    def step(S, inp):
        r_t, w_t, k_t, v_t, a_t, b_t = inp
        sa = jnp.einsum("bhvk,bhk->bhv", S, a_t)
        S = S * w_t[:, :, None, :] + sa[..., None] * b_t[:, :, None, :] + v_t[..., None] * k_t[:, :, None, :]
        return S, jnp.einsum("bhvk,bhk->bhv", S, r_t)
    xs = tuple(jnp.swapaxes(t, 0, 1) for t in (r, w, k, v, a, b))
    S, ys = lax.scan(step, S0, xs)
    return jnp.swapaxes(ys, 0, 1), S


def rwkv_mixer(p, shift_prev, S0, lw):
    B, T, _ = p.shape
    f32 = jnp.float32
    prev = jnp.concatenate([shift_prev[:, None, :].astype(p.dtype), p[:, :-1]], axis=1)
    xs = p.astype(f32) + lw["rwkv_mu"].astype(f32) * (prev.astype(f32) - p.astype(f32))
    r, k, v, wl, al, gl = jnp.split(xs, RWKV_SPLITS, axis=-1)
    log_w = -jax.nn.softplus(-(lw["rwkv_w0"].astype(f32) + jnp.tanh(wl) @ lw["rwkv_w2"].astype(f32))) - 0.5
    decay = jnp.exp(-jnp.exp(log_w))
    a = jax.nn.sigmoid(lw["rwkv_a0"].astype(f32) + al @ lw["rwkv_a2"].astype(f32))
    g = jax.nn.sigmoid(gl) @ lw["rwkv_g2"].astype(f32)
    heads = lambda t: t.reshape(B, T, RWKV_HEADS, RWKV_HEAD)
    kk = heads(k * lw["rwkv_kk"].astype(f32))
    kk = kk * lax.rsqrt(jnp.maximum(jnp.sum(kk * kk, -1, keepdims=True), 1e-24))
    k = k * (1.0 + (a - 1.0) * lw["rwkv_ka"].astype(f32))
    r_h, k_h, v_h, a_h = heads(r), heads(k), heads(v), heads(a)
    y, S = wkv7_scan(S0.astype(f32), r_h, heads(decay), k_h, v_h, -kk, kk * a_h)
    mu_y = jnp.mean(y, -1, keepdims=True)
    var_y = jnp.mean(jnp.square(y - mu_y), -1, keepdims=True)
    y = ((y - mu_y) * lax.rsqrt(var_y + GN_EPS)).reshape(B, T, MIX_W)
    y = y * lw["rwkv_gn_w"].astype(f32) + lw["rwkv_gn_b"].astype(f32)
    bonus = (jnp.sum(r_h * k_h * lw["rwkv_rk"].astype(f32), -1, keepdims=True) * v_h).reshape(B, T, MIX_W)
    o = (y + bonus) * g
    return o.astype(p.dtype), p[:, -1], S.astype(S0.dtype)


def gmlp_mixer(p, lw):
    B, T, _ = p.shape
    u = jax.nn.gelu(p[..., :MIX_W])
    v = layer_norm(jax.nn.gelu(p[..., MIX_W:]), lw["gmlp_ln_w"], lw["gmlp_ln_b"])
    L = min(T, GMLP_CHUNK)
    mask = jnp.tril(jnp.ones((L, L), dtype=bool))
    w_s = jnp.where(mask[None], lw["gmlp_ws"][:, :L, :L], 0)
    vc = v.reshape(B, T // L, L, GMLP_GROUPS, GMLP_GDIM)
    mixed = jnp.einsum("gij,bcjgd->bcigd", w_s, vc) + lw["gmlp_bs"][:, :L].T[None, None, :, :, None]
    return u * mixed.reshape(B, T, MIX_W), v


def conv_mixer(p, conv_prev, lw):
    z = p[..., :MIX_W] * jax.nn.sigmoid(p[..., MIX_W:])
    zp = jnp.concatenate([conv_prev.astype(z.dtype), z], axis=1)
    y = lax.conv_general_dilated(zp, lw["conv_dw"][:, None, :].astype(z.dtype), window_strides=(1,),
                                 padding="VALID", dimension_numbers=("NWC", "WIO", "NWC"),
                                 feature_group_count=MIX_W) + lw["conv_dw_b"].astype(z.dtype)
    y = jax.nn.silu(layer_norm(y, lw["conv_ln_w"], lw["conv_ln_b"]))
    return y, zp[:, -(CONV_W - 1):]


def rel_bias(table, q_off, tq, tk):
    rel = (q_off + jnp.arange(tq))[:, None] - jnp.arange(tk)[None, :]
    return table[:, jnp.clip(rel, -REL_CLIP, REL_CLIP) + REL_CLIP].astype(jnp.float32)


def band_attention_prompt(q, k, v, table):
    B, T, H, Dh = q.shape
    NC = T // CHUNK
    NB = BAND_PAST_CHUNKS + 1
    pad = ((0, 0), (BAND_PAST, 0), (0, 0), (0, 0))
    idx = jnp.arange(NC)[:, None] + jnp.arange(NB)[None, :]
    kb = jnp.pad(k, pad).reshape(B, NC + BAND_PAST_CHUNKS, CHUNK, H, Dh)[:, idx].reshape(B, NC, NB * CHUNK, H, Dh)
    vb = jnp.pad(v, pad).reshape(B, NC + BAND_PAST_CHUNKS, CHUNK, H, Dh)[:, idx].reshape(B, NC, NB * CHUNK, H, Dh)
    qc = q.reshape(B, NC, CHUNK, H, Dh)
    s = jnp.einsum("bcqhd,bckhd->bchqk", qc, kb).astype(jnp.float32) * (Dh ** -0.5)
    s = s + rel_bias(table, BAND_PAST, CHUNK, NB * CHUNK)[None, None]
    valid = (jnp.arange(NC)[:, None] * CHUNK + jnp.arange(NB * CHUNK)[None, :]) >= BAND_PAST
    s = jnp.where(valid[None, :, None, None, :], s, -1e30)
    pr = jax.nn.softmax(s, axis=-1).astype(v.dtype)
    return jnp.einsum("bchqk,bckhd->bcqhd", pr, vb).reshape(B, T, H * Dh)


def band_attention_sample(q, k, v, cache_k, cache_v, table):
    B, T, H, Dh = q.shape
    L = cache_k.shape[1]
    kb = jnp.concatenate([cache_k.astype(k.dtype), k], axis=1)
    vb = jnp.concatenate([cache_v.astype(v.dtype), v], axis=1)
    s = jnp.einsum("bqhd,bkhd->bhqk", q, kb).astype(jnp.float32) * (Dh ** -0.5)
    s = s + rel_bias(table, L, T, L + T)[None]
    pr = jax.nn.softmax(s, axis=-1).astype(v.dtype)
    return jnp.einsum("bhqk,bkhd->bqhd", pr, vb).reshape(B, T, H * Dh)


def hybrid_layer(x, lw, rwkv_shift0, rwkv_S0, conv0, kv_cache):
    B, T, _ = x.shape
    h = rms_norm(x, lw["norm_mix"])
    proj = h @ lw["w_in"]
    o_rwkv, shift_new, S_new = rwkv_mixer(proj[..., OFF_RWKV:OFF_GMLP], rwkv_shift0, rwkv_S0, lw)
    o_gmlp, v_gmlp = gmlp_mixer(proj[..., OFF_GMLP:OFF_CONV], lw)
    o_conv, conv_new = conv_mixer(proj[..., OFF_CONV:OFF_ATTN], conv0, lw)
    qkv = proj[..., OFF_ATTN:OFF_GATE].reshape(B, T, 3, ATTN_HEADS, ATTN_HEAD)
    q, k, v = qkv[:, :, 0], qkv[:, :, 1], qkv[:, :, 2]
    if kv_cache is None:
        o_attn = band_attention_prompt(q, k, v, lw["attn_rel_bias"])
        rows = min(BAND_PAST, T)
        k_new, v_new = k[:, T - rows:], v[:, T - rows:]
    else:
        o_attn = band_attention_sample(q, k, v, kv_cache[0], kv_cache[1], lw["attn_rel_bias"])
        k_new, v_new = k, v
    gates = jax.nn.sigmoid(proj[..., OFF_GATE:])
    branches = (o_rwkv, o_gmlp, o_conv, o_attn)
    merged = gates[..., :D_MODEL] * (branches[0] @ lw["w_branch"][0])
    for n in range(1, N_BRANCH):
        merged = merged + gates[..., n * D_MODEL:(n + 1) * D_MODEL] * (branches[n] @ lw["w_branch"][n])
    x = x + merged @ lw["w_out"]
    h2 = rms_norm(x, lw["norm_ffn"])
    gu = h2 @ lw["w_ffn_in"]
    x = x + (jax.nn.silu(gu[..., :D_FF]) * gu[..., D_FF:]) @ lw["w_ffn_out"]
    return x, (shift_new, S_new, conv_new, k_new, v_new, v_gmlp)


def setup_inputs(seed: int = 0) -> dict:
    key = jax.random.key(seed)
    ks = iter(jax.random.split(key, 40))
    nrm = lambda shape, scale: scale * jax.random.normal(next(ks), shape, jnp.float32)
    uni = lambda shape, lo, hi: jax.random.uniform(next(ks), shape, jnp.float32, lo, hi)
    attn_rows = min(BAND_PAST, PAST_LEN)
    return {
        "x_prompt": nrm((BATCH, SEQ, D_MODEL), 1.0),
        "x_sample": nrm((DEC_BATCH, DEC_SEQ, D_MODEL), 1.0),
        "state_rwkv_shift": nrm((DEPTH, DEC_BATCH, RWKV_COLS), 1.0),
        "state_rwkv_wkv": nrm((DEPTH, DEC_BATCH, RWKV_HEADS, RWKV_HEAD, RWKV_HEAD), 0.5),
        "cache_conv": nrm((DEPTH, DEC_BATCH, CONV_W - 1, MIX_W), 0.5),
        "cache_attn_k": nrm((DEPTH, DEC_BATCH, attn_rows, ATTN_HEADS, ATTN_HEAD), 1.0),
        "cache_attn_v": nrm((DEPTH, DEC_BATCH, attn_rows, ATTN_HEADS, ATTN_HEAD), 1.0),
        "norm_mix": 1.0 + nrm((DEPTH, D_MODEL), 0.02),
        "norm_ffn": 1.0 + nrm((DEPTH, D_MODEL), 0.02),
        "norm_final": 1.0 + nrm((D_MODEL,), 0.02),
        "w_in": nrm((DEPTH, D_MODEL, IN_COLS), D_MODEL ** -0.5),
        "rwkv_mu": uni((DEPTH, RWKV_COLS), 0.0, 1.0),
        "rwkv_w0": uni((DEPTH, MIX_W), -2.5, 1.5),
        "rwkv_w2": nrm((DEPTH, W_RANK, MIX_W), 0.5 * W_RANK ** -0.5),
        "rwkv_a0": nrm((DEPTH, MIX_W), 0.1),
        "rwkv_a2": nrm((DEPTH, A_RANK, MIX_W), 0.5 * A_RANK ** -0.5),
        "rwkv_g2": nrm((DEPTH, G_RANK, MIX_W), G_RANK ** -0.5),
        "rwkv_kk": 0.85 + nrm((DEPTH, MIX_W), 0.02),
        "rwkv_ka": 1.0 + nrm((DEPTH, MIX_W), 0.02),
        "rwkv_rk": nrm((DEPTH, RWKV_HEADS, RWKV_HEAD), 0.1),
        "rwkv_gn_w": 1.0 + nrm((DEPTH, MIX_W), 0.02),
        "rwkv_gn_b": nrm((DEPTH, MIX_W), 0.02),
        "gmlp_ln_w": 1.0 + nrm((DEPTH, MIX_W), 0.02),
        "gmlp_ln_b": nrm((DEPTH, MIX_W), 0.02),
        "gmlp_ws": nrm((DEPTH, GMLP_GROUPS, GMLP_CHUNK, GMLP_CHUNK), 0.5 * GMLP_CHUNK ** -0.5),
        "gmlp_bs": 1.0 + nrm((DEPTH, GMLP_GROUPS, GMLP_CHUNK), 0.1),
        "conv_dw": nrm((DEPTH, CONV_W, MIX_W), CONV_W ** -0.5),
        "conv_dw_b": nrm((DEPTH, MIX_W), 0.02),
        "conv_ln_w": 1.0 + nrm((DEPTH, MIX_W), 0.02),
        "conv_ln_b": nrm((DEPTH, MIX_W), 0.02),
        "attn_rel_bias": nrm((DEPTH, ATTN_HEADS, 2 * REL_CLIP + 1), 0.2),
        "w_branch": nrm((DEPTH, N_BRANCH, MIX_W, D_MODEL), MIX_W ** -0.5),
        "w_out": nrm((DEPTH, D_MODEL, D_MODEL), 0.5 * D_MODEL ** -0.5),
        "w_ffn_in": nrm((DEPTH, D_MODEL, 2 * D_FF), D_MODEL ** -0.5),
        "w_ffn_out": nrm((DEPTH, D_FF, D_MODEL), 0.5 * D_FF ** -0.5),
    }


def reference(x_prompt, x_sample, state_rwkv_shift, state_rwkv_wkv, cache_conv, cache_attn_k, cache_attn_v,
              norm_mix, norm_ffn, norm_final, w_in, rwkv_mu, rwkv_w0, rwkv_w2, rwkv_a0, rwkv_a2, rwkv_g2,
              rwkv_kk, rwkv_ka, rwkv_rk, rwkv_gn_w, rwkv_gn_b, gmlp_ln_w, gmlp_ln_b, gmlp_ws, gmlp_bs,
              conv_dw, conv_dw_b, conv_ln_w, conv_ln_b, attn_rel_bias, w_branch, w_out, w_ffn_in, w_ffn_out):
    B = x_prompt.shape[0]
    dt = x_prompt.dtype
    xp, xs = x_prompt, x_sample
    p_states = ([], [], [], [], [])
    s_states = ([], [], [], [], [], [])
    for l in range(DEPTH):
        lw = {
            "norm_mix": norm_mix[l], "norm_ffn": norm_ffn[l], "w_in": w_in[l],
            "rwkv_mu": rwkv_mu[l], "rwkv_w0": rwkv_w0[l], "rwkv_w2": rwkv_w2[l], "rwkv_a0": rwkv_a0[l],
            "rwkv_a2": rwkv_a2[l], "rwkv_g2": rwkv_g2[l], "rwkv_kk": rwkv_kk[l], "rwkv_ka": rwkv_ka[l],
            "rwkv_rk": rwkv_rk[l], "rwkv_gn_w": rwkv_gn_w[l], "rwkv_gn_b": rwkv_gn_b[l],
            "gmlp_ln_w": gmlp_ln_w[l], "gmlp_ln_b": gmlp_ln_b[l], "gmlp_ws": gmlp_ws[l], "gmlp_bs": gmlp_bs[l],
            "conv_dw": conv_dw[l], "conv_dw_b": conv_dw_b[l], "conv_ln_w": conv_ln_w[l], "conv_ln_b": conv_ln_b[l],
            "attn_rel_bias": attn_rel_bias[l], "w_branch": w_branch[l], "w_out": w_out[l],
            "w_ffn_in": w_ffn_in[l], "w_ffn_out": w_ffn_out[l],
        }
        xp, st = hybrid_layer(xp, lw,
                              jnp.zeros((B, RWKV_COLS), dt),
                              jnp.zeros((B, RWKV_HEADS, RWKV_HEAD, RWKV_HEAD), dt),
                              jnp.zeros((B, CONV_W - 1, MIX_W), dt),
                              None)
        for lst, val in zip(p_states, st[:5]):
            lst.append(val)
        xs, st = hybrid_layer(xs, lw, state_rwkv_shift[l], state_rwkv_wkv[l], cache_conv[l],
                              (cache_attn_k[l], cache_attn_v[l]))
        for lst, val in zip(s_states, st):
            lst.append(val)
    y_prompt = rms_norm(xp, norm_final)
    y_sample = rms_norm(xs, norm_final)
    p_rwkv_shift, p_rwkv_wkv, p_conv, p_attn_k, p_attn_v = [jnp.stack(v_) for v_ in p_states]
    s_rwkv_shift, s_rwkv_wkv, s_conv, s_attn_k, s_attn_v, s_gmlp_v = [jnp.stack(v_) for v_ in s_states]
    return (y_prompt, y_sample, p_rwkv_shift, p_rwkv_wkv, p_conv, p_attn_k, p_attn_v,
            s_rwkv_shift, s_rwkv_wkv, s_conv, s_attn_k, s_attn_v, s_gmlp_v)
```

```python
import functools

import jax
import jax.numpy as jnp
from jax import lax
from jax.experimental import pallas as pl
from jax.experimental.pallas import tpu as pltpu

F32 = jnp.float32
BF16 = jnp.bfloat16

MIX_W = 512
N_BRANCH = 4
HEADS = 8
HEAD = 64
W_RANK = 96
A_RANK = 96
G_RANK = 256
LORA_PAD = 128
RWKV_COLS = 3 * MIX_W + W_RANK + A_RANK + G_RANK
RWKV_PAD_COLS = 3 * MIX_W + 2 * LORA_PAD + G_RANK
GN_EPS = 64e-5
RMS_EPS = 1e-6
LN_EPS = 1e-5
CHUNK = 64
GMLP_CHUNK = 128
CONV_W = 31
CONV_TAIL = 32
BAND_PAST = 8 * CHUNK
BAND = BAND_PAST + CHUNK
REL_CLIP = 128
REL_PAD = 384
MIX_COLS = RWKV_PAD_COLS + 2 * MIX_W + 2 * MIX_W + 3 * MIX_W
COL_GMLP = RWKV_PAD_COLS // (2 * MIX_W)
COL_CONV = COL_GMLP + 1
COL_Q = (RWKV_PAD_COLS + 4 * MIX_W) // MIX_W

VMEM_LIMIT_BYTES = 52 * 1024 * 1024
ROW_TILE_TARGET = 768
COL_TILE_TARGET = 512


def _pick_tile(n, target, mult):
    best = None
    for t in range(mult, min(n, target) + 1, mult):
        if n % t == 0:
            best = t
    assert best is not None, (n, target, mult)
    return best


def _params(sem):
    return pltpu.CompilerParams(dimension_semantics=sem, vmem_limit_bytes=VMEM_LIMIT_BYTES)


def _rms_bf16(x, g):
    ms = jnp.mean(x * x, axis=-1, keepdims=True)
    return (x * lax.rsqrt(ms + RMS_EPS) * g).astype(BF16)


def _layer_norm(x, w, b):
    mu = jnp.mean(x, axis=-1, keepdims=True)
    var = jnp.mean(jnp.square(x - mu), axis=-1, keepdims=True)
    return (x - mu) * lax.rsqrt(var + LN_EPS) * w + b


def _gelu_tanh(x):
    return 0.5 * x * (1.0 + jnp.tanh(0.7978845608028654 * (x + 0.044715 * x * x * x)))


def _dot(a, b):
    return jnp.dot(a, b, preferred_element_type=F32)


def _dot_nt(a, b):
    return lax.dot_general(a, b, (((1,), (1,)), ((), ())), preferred_element_type=F32)


def _dot_tn(a, b):
    return lax.dot_general(a, b, (((0,), (0,)), ((), ())), preferred_element_type=F32)


def _inproj_kernel(x_ref, g_ref, w_ref, proj_ref, h_ref):
    @pl.when(pl.program_id(1) == 0)
    def _():
        h_ref[...] = _rms_bf16(x_ref[...], g_ref[...])

    proj_ref[...] = _dot(h_ref[...], w_ref[...])


def _inproj(x, g, w_mix, layer):
    m, d = x.shape
    n = w_mix.shape[-1]
    tm = _pick_tile(m, ROW_TILE_TARGET, 16)
    tn = _pick_tile(n, COL_TILE_TARGET, 128)
    return pl.pallas_call(
        _inproj_kernel,
        grid=(m // tm, n // tn),
        in_specs=[
            pl.BlockSpec((tm, d), lambda i, j: (i, 0)),
            pl.BlockSpec((None, 1, d), lambda i, j: (layer, 0, 0)),
            pl.BlockSpec((None, d, tn), lambda i, j: (layer, 0, j)),
        ],
        out_specs=[
            pl.BlockSpec((tm, tn), lambda i, j: (i, j)),
            pl.BlockSpec((tm, d), lambda i, j: (i, 0)),
        ],
        out_shape=[jax.ShapeDtypeStruct((m, n), F32), jax.ShapeDtypeStruct((m, d), BF16)],
        compiler_params=_params(("parallel", "arbitrary")),
        name="inproj",
    )(x, g, w_mix)


def _merge_kernel(h_ref, wg0, wg1, wg2, wg3, b0, b1, b2, b3, wb_ref, o_ref):
    h = h_ref[...]
    acc = None
    for n, (wg, br) in enumerate(((wg0, b0), (wg1, b1), (wg2, b2), (wg3, b3))):
        gate = jax.nn.sigmoid(_dot(h, wg[...]))
        term = gate * _dot(br[...], wb_ref[n])
        acc = term if acc is None else acc + term
    o_ref[...] = acc.astype(BF16)


def _merge(h, w_gate, branches, w_branch, layer):
    m, d = h.shape
    tm = _pick_tile(m, ROW_TILE_TARGET, 16)
    tn = _pick_tile(d, COL_TILE_TARGET, 128)
    nj = d // tn
    gate_specs = [pl.BlockSpec((None, d, tn), functools.partial(lambda i, j, n: (layer, 0, n * nj + j), n=n))
                  for n in range(N_BRANCH)]
    br_specs = [pl.BlockSpec((tm, MIX_W), lambda i, j: (i, 0)) for _ in range(N_BRANCH)]
    return pl.pallas_call(
        _merge_kernel,
        grid=(m // tm, nj),
        in_specs=[pl.BlockSpec((tm, d), lambda i, j: (i, 0))] + gate_specs + br_specs
        + [pl.BlockSpec((None, N_BRANCH, MIX_W, tn), lambda i, j: (layer, 0, 0, j))],
        out_specs=pl.BlockSpec((tm, tn), lambda i, j: (i, j)),
        out_shape=jax.ShapeDtypeStruct((m, d), BF16),
        compiler_params=_params(("parallel", "arbitrary")),
        name="merge",
    )(h, w_gate, w_gate, w_gate, w_gate, *branches, w_branch)


def _resid_kernel(a_ref, w_ref, x_ref, o_ref):
    o_ref[...] = x_ref[...] + _dot(a_ref[...], w_ref[...])


def _resid(a, w, x, layer):
    m, k = a.shape
    d = x.shape[-1]
    tm = _pick_tile(m, ROW_TILE_TARGET, 16)
    tn = _pick_tile(d, COL_TILE_TARGET, 128)
    return pl.pallas_call(
        _resid_kernel,
        grid=(m // tm, d // tn),
        in_specs=[
            pl.BlockSpec((tm, k), lambda i, j: (i, 0)),
            pl.BlockSpec((None, k, tn), lambda i, j: (layer, 0, j)),
            pl.BlockSpec((tm, tn), lambda i, j: (i, j)),
        ],
        out_specs=pl.BlockSpec((tm, tn), lambda i, j: (i, j)),
        out_shape=jax.ShapeDtypeStruct((m, d), F32),
        compiler_params=_params(("parallel", "arbitrary")),
        name="resid",
    )(a, w, x)


def _ffn_in_kernel(x_ref, g_ref, wg_ref, wu_ref, act_ref, h_scr):
    @pl.when(pl.program_id(1) == 0)
    def _():
        h_scr[...] = _rms_bf16(x_ref[...], g_ref[...])

    h = h_scr[...]
    act_ref[...] = (jax.nn.silu(_dot(h, wg_ref[...])) * _dot(h, wu_ref[...])).astype(BF16)


def _ffn_in(x, g, w_ffn_in, layer):
    m, d = x.shape
    dff = w_ffn_in.shape[-1] // 2
    tm = _pick_tile(m, ROW_TILE_TARGET, 16)
    tn = _pick_tile(dff, COL_TILE_TARGET, 128)
    nj = dff // tn
    return pl.pallas_call(
        _ffn_in_kernel,
        grid=(m // tm, nj),
        in_specs=[
            pl.BlockSpec((tm, d), lambda i, j: (i, 0)),
            pl.BlockSpec((None, 1, d), lambda i, j: (layer, 0, 0)),
            pl.BlockSpec((None, d, tn), lambda i, j: (layer, 0, j)),
            pl.BlockSpec((None, d, tn), lambda i, j: (layer, 0, nj + j)),
        ],
        out_specs=pl.BlockSpec((tm, tn), lambda i, j: (i, j)),
        out_shape=jax.ShapeDtypeStruct((m, dff), BF16),
        scratch_shapes=[pltpu.VMEM((tm, d), BF16)],
        compiler_params=_params(("parallel", "arbitrary")),
        name="ffn_in",
    )(x, g, w_ffn_in, w_ffn_in)


def _final_norm_kernel(x_ref, g_ref, o_ref):
    x = x_ref[...]
    ms = jnp.mean(x * x, axis=-1, keepdims=True)
    o_ref[...] = x * lax.rsqrt(ms + RMS_EPS) * g_ref[...]


def _final_norm(x, g):
    m, d = x.shape
    tm = _pick_tile(m, ROW_TILE_TARGET, 16)
    return pl.pallas_call(
        _final_norm_kernel,
        grid=(m // tm,),
        in_specs=[pl.BlockSpec((tm, d), lambda i: (i, 0)), pl.BlockSpec((1, d), lambda i: (0, 0))],
        out_specs=pl.BlockSpec((tm, d), lambda i: (i, 0)),
        out_shape=jax.ShapeDtypeStruct((m, d), F32),
        compiler_params=_params(("parallel",)),
        name="final_norm",
    )(x, g)


def _split3(x):
    hi = x.astype(BF16)
    r1 = x - hi.astype(F32)
    mid = r1.astype(BF16)
    lo = (r1 - mid.astype(F32)).astype(BF16)
    return hi, mid, lo


def _head_sum(x, ones_bd):
    rows = x.shape[0]
    s = _dot(jnp.concatenate(_split3(x), axis=0), ones_bd)
    return s[:rows] + s[rows:2 * rows] + s[2 * rows:]


def _rwkv_kernel(p_ref, shift0_ref, s0_ref, mu_ref, w0_ref, w2_ref, a0_ref, a2_ref, g2_ref, kkw_ref, ka_ref,
                 rk_ref, gnw_ref, gnb_ref, ones_ref, tri_ref, o_ref, sout_ref, carry_ref, s_ref, y_ref, *, nc):
    L = p_ref.shape[0]
    c = pl.program_id(1)

    @pl.when(c == 0)
    def _():
        carry_ref[...] = shift0_ref[...]
        s_ref[...] = s0_ref[...]

    p = p_ref[...]
    row = lax.broadcasted_iota(jnp.int32, p.shape, 0)
    prev = jnp.where(row == 0, carry_ref[...], pltpu.roll(p, 1, 0))
    carry_ref[...] = p[L - 1:L, :]
    xs = p + mu_ref[...] * (prev - p)

    o_w = 3 * MIX_W
    r = xs[:, 0:MIX_W]
    k = xs[:, MIX_W:2 * MIX_W]
    v = xs[:, 2 * MIX_W:3 * MIX_W]
    wl = xs[:, o_w:o_w + LORA_PAD]
    al = xs[:, o_w + LORA_PAD:o_w + 2 * LORA_PAD]
    gl = xs[:, o_w + 2 * LORA_PAD:]
    ones_bd = ones_ref[...]

    zz = w0_ref[...] + _dot(jnp.tanh(wl).astype(BF16), w2_ref[...])
    nz = -zz
    softplus = jnp.maximum(nz, 0.0) + jnp.log(1.0 + jnp.exp(-jnp.abs(nz)))
    lw = -jnp.exp(-softplus - 0.5)
    a = jax.nn.sigmoid(a0_ref[...] + _dot(al.astype(BF16), a2_ref[...]))
    g = _dot(jax.nn.sigmoid(gl).astype(BF16), g2_ref[...])
    kk = k * kkw_ref[...]
    kk = kk * lax.rsqrt(jnp.maximum(_head_sum(kk * kk, ones_bd), 1e-24))
    k = k * (1.0 + (a - 1.0) * ka_ref[...])
    a_s = -kk
    b_s = kk * a
    bonus = _head_sum(r * k * rk_ref[...], ones_bd) * v

    cum3 = _dot(tri_ref[...], jnp.concatenate(_split3(lw), axis=1))
    cum = cum3[:, :MIX_W] + cum3[:, MIX_W:2 * MIX_W] + cum3[:, 2 * MIX_W:]
    cl = cum[L - 1:L, :]
    inv_p = jnp.exp(-cum)
    to_end = jnp.exp(cl - cum)
    at = (a_s * jnp.exp(cum - lw)).astype(BF16)
    bt = (b_s * inv_p).astype(BF16)
    kt = (k * inv_p).astype(BF16)
    rt = (r * jnp.exp(cum)).astype(BF16)
    bh = (b_s * to_end).astype(BF16)
    kh = (k * to_end).astype(BF16)
    p_end = jnp.exp(cl)
    vb = v.astype(BF16)

    ri = lax.broadcasted_iota(jnp.int32, (L, 2 * L), 0)
    ci = lax.broadcasted_iota(jnp.int32, (L, 2 * L), 1)
    ci = jnp.where(ci >= L, ci - L, ci)
    strict = ci < ri
    incl = ci <= ri
    eye = (lax.broadcasted_iota(jnp.int32, (L, L), 0) == lax.broadcasted_iota(jnp.int32, (L, L), 1)).astype(F32)

    for h in range(HEADS):
        hs = slice(h * HEAD, (h + 1) * HEAD)
        at_h, rt_h, v_h = at[:, hs], rt[:, hs], vb[:, hs]
        gm = _dot_nt(jnp.concatenate([at_h, rt_h], axis=0), jnp.concatenate([bt[:, hs], kt[:, hs]], axis=0))
        a_top = jnp.where(strict, gm[:L], 0.0)
        m_low = jnp.where(incl, gm[L:], 0.0).astype(BF16)
        a_ab = a_top[:, :L]
        a_ak = a_top[:, L:]
        t_inv = eye + a_ab
        pw = a_ab
        n = 1
        while 2 * n < L:
            pwb = pw.astype(BF16)
            pw = _dot(pwb, pwb)
            t_inv = t_inv + _dot(t_inv.astype(BF16), pw.astype(BF16))
            n *= 2
        tb = t_inv.astype(BF16)
        u_free = _dot(tb, _dot(a_ak.astype(BF16), v_h).astype(BF16))
        w_m = _dot(tb, at_h)
        s_h = s_ref[h]
        ws = _dot_nt(jnp.concatenate([w_m.astype(BF16), rt_h], axis=0), s_h.astype(BF16))
        u = ws[:L] + u_free
        uv = jnp.concatenate([u.astype(BF16), v_h], axis=0)
        y_ref[:, hs] = ws[L:] + _dot(m_low, uv)
        s_ref[h] = s_h * p_end[:, hs] + _dot_tn(uv, jnp.concatenate([bh[:, hs], kh[:, hs]], axis=0))

    y = y_ref[...]
    mu_y = _head_sum(y, ones_bd) * (1.0 / HEAD)
    yc = y - mu_y
    var_y = _head_sum(yc * yc, ones_bd) * (1.0 / HEAD)
    yn = yc * lax.rsqrt(var_y + GN_EPS) * gnw_ref[...] + gnb_ref[...]
    o_ref[...] = ((yn + bonus) * g).astype(BF16)

    @pl.when(c == nc - 1)
    def _():
        sout_ref[...] = s_ref[...]


def _rwkv(proj, out_prev, shift0, s0, lp, layer, *, row0, batch, seq, chunk):
    nc = seq // chunk
    rb0 = row0 // chunk
    vec = lambda name: pl.BlockSpec((None, 1, lp[name].shape[-1]), lambda b, c: (layer, 0, 0))
    mat = lambda name: pl.BlockSpec((None,) + lp[name].shape[1:], lambda b, c: (layer, 0, 0))
    const = lambda arr: pl.BlockSpec(arr.shape, lambda b, c: (0,) * arr.ndim)
    tri = jnp.tril(jnp.ones((chunk, chunk), BF16))
    ones_bd = lp["ones_bd"]
    in_specs = [
        pl.BlockSpec((chunk, RWKV_PAD_COLS), lambda b, c: (rb0 + b * nc + c, 0)),
        pl.BlockSpec((None, 1, RWKV_PAD_COLS), lambda b, c: (b, 0, 0)),
        pl.BlockSpec((None, HEADS, HEAD, HEAD), lambda b, c: (b, 0, 0, 0)),
        vec("rwkv_mu"), vec("rwkv_w0"), mat("rwkv_w2"), vec("rwkv_a0"), mat("rwkv_a2"), mat("rwkv_g2"),
        vec("rwkv_kk"), vec("rwkv_ka"), vec("rwkv_rk"), vec("rwkv_gn_w"), vec("rwkv_gn_b"),
        const(ones_bd), const(tri),
    ]
    args = [proj, shift0, s0, lp["rwkv_mu"], lp["rwkv_w0"], lp["rwkv_w2"], lp["rwkv_a0"], lp["rwkv_a2"],
            lp["rwkv_g2"], lp["rwkv_kk"], lp["rwkv_ka"], lp["rwkv_rk"], lp["rwkv_gn_w"], lp["rwkv_gn_b"],
            ones_bd, tri]
    out_shape = [jax.ShapeDtypeStruct((proj.shape[0], MIX_W), BF16),
                 jax.ShapeDtypeStruct((batch, HEADS, HEAD, HEAD), F32)]
    aliases = {}
    if out_prev is not None:
        in_specs.append(pl.BlockSpec(memory_space=pl.ANY))
        args.append(out_prev)
        aliases = {len(args) - 1: 0}
    kern = functools.partial(_rwkv_kernel, nc=nc)
    if out_prev is not None:
        kern = _drop_alias_arg(kern, n_in=len(args) - 1)
    return pl.pallas_call(
        kern,
        grid=(batch, nc),
        in_specs=in_specs,
        out_specs=[pl.BlockSpec((chunk, MIX_W), lambda b, c: (rb0 + b * nc + c, 0)),
                   pl.BlockSpec((None, HEADS, HEAD, HEAD), lambda b, c: (b, 0, 0, 0))],
        out_shape=out_shape,
        scratch_shapes=[pltpu.VMEM((1, RWKV_PAD_COLS), F32), pltpu.VMEM((HEADS, HEAD, HEAD), F32),
                        pltpu.VMEM((chunk, MIX_W), F32)],
        input_output_aliases=aliases,
        compiler_params=_params(("parallel", "arbitrary")),
        name="rwkv",
    )(*args)


def _drop_alias_arg(kern, n_in):
    def wrapped(*refs):
        return kern(*refs[:n_in], *refs[n_in + 1:])
    return wrapped


def _gmlp_kernel(p_ref, lnw_ref, lnb_ref, ws_ref, bias_ref, o_ref, *rest, emit_v):
    if emit_v:
        v_ref, m_ref = rest
    else:
        (m_ref,) = rest
    L = p_ref.shape[0]
    p = p_ref[...]
    u = _gelu_tanh(p[:, :MIX_W])
    v = _layer_norm(_gelu_tanh(p[:, MIX_W:]), lnw_ref[...], lnb_ref[...])
    if emit_v:
        v_ref[...] = v
    vb = v.astype(BF16)
    lower = lax.broadcasted_iota(jnp.int32, (L, L), 1) <= lax.broadcasted_iota(jnp.int32, (L, L), 0)
    for g in range(HEADS):
        gs = slice(g * HEAD, (g + 1) * HEAD)
        w = jnp.where(lower, ws_ref[g, :L, :L], 0.0).astype(BF16)
        m_ref[:, gs] = _dot(w, vb[:, gs])
    o_ref[...] = (u * (m_ref[...] + bias_ref[...])).astype(BF16)


def _gmlp(proj, out_prev, lp, layer, *, row0, batch, seq, chunk, emit_v):
    nc = seq // chunk
    rb0 = row0 // chunk
    n_rows = batch * seq
    bias = lp["gmlp_bias"][:, :chunk, :]
    in_specs = [
        pl.BlockSpec((chunk, 2 * MIX_W), lambda i: (rb0 + i, COL_GMLP)),
        pl.BlockSpec((None, 1, MIX_W), lambda i: (layer, 0, 0)),
        pl.BlockSpec((None, 1, MIX_W), lambda i: (layer, 0, 0)),
        pl.BlockSpec((None, HEADS, GMLP_CHUNK, GMLP_CHUNK), lambda i: (layer, 0, 0, 0)),
        pl.BlockSpec((None, chunk, MIX_W), lambda i: (layer, 0, 0)),
    ]
    args = [proj, lp["gmlp_ln_w"], lp["gmlp_ln_b"], lp["gmlp_ws"], bias]
    out_specs = [pl.BlockSpec((chunk, MIX_W), lambda i: (rb0 + i, 0))]
    out_shape = [jax.ShapeDtypeStruct((proj.shape[0], MIX_W), BF16)]
    if emit_v:
        out_specs.append(pl.BlockSpec((chunk, MIX_W), lambda i: (i, 0)))
        out_shape.append(jax.ShapeDtypeStruct((n_rows, MIX_W), F32))
    kern = functools.partial(_gmlp_kernel, emit_v=emit_v)
    aliases = {}
    if out_prev is not None:
        in_specs.append(pl.BlockSpec(memory_space=pl.ANY))
        args.append(out_prev)
        aliases = {len(args) - 1: 0}
        kern = _drop_alias_arg(kern, n_in=len(args) - 1)
    return pl.pallas_call(
        kern,
        grid=(batch * nc,),
        in_specs=in_specs,
        out_specs=out_specs,
        out_shape=out_shape,
        scratch_shapes=[pltpu.VMEM((chunk, MIX_W), F32)],
        input_output_aliases=aliases,
        compiler_params=_params(("parallel",)),
        name="gmlp",
    )(*args)


def _conv_kernel(p_ref, prev_ref, dw_ref, dwb_ref, lnw_ref, lnb_ref, o_ref, state_ref, buf_ref, *, nt):
    tb = p_ref.shape[0]
    t = pl.program_id(1)

    @pl.when(t == 0)
    def _():
        buf_ref[0:CONV_TAIL, :] = prev_ref[...]

    @pl.when(t > 0)
    def _():
        buf_ref[0:CONV_TAIL, :] = buf_ref[tb:tb + CONV_TAIL, :]

    p = p_ref[...]
    buf_ref[CONV_TAIL:CONV_TAIL + tb, :] = p[:, :MIX_W] * jax.nn.sigmoid(p[:, MIX_W:])
    sub = min(tb, 32)
    off = CONV_TAIL - (CONV_W - 1)
    for r0 in range(0, tb, sub):
        acc = None
        for w in range(CONV_W):
            term = buf_ref[r0 + w + off:r0 + w + off + sub, :] * dw_ref[w:w + 1, :]
            acc = term if acc is None else acc + term
        y = _layer_norm(acc + dwb_ref[...], lnw_ref[...], lnb_ref[...])
        o_ref[r0:r0 + sub, :] = jax.nn.silu(y).astype(BF16)

    @pl.when(t == nt - 1)
    def _():
        state_ref[...] = buf_ref[tb + off:tb + CONV_TAIL, :]


def _conv(proj, out_prev, prev, lp, layer, *, row0, batch, seq, tb):
    nt = seq // tb
    rb0 = row0 // tb
    in_specs = [
        pl.BlockSpec((tb, 2 * MIX_W), lambda b, t: (rb0 + b * nt + t, COL_CONV)),
        pl.BlockSpec((None, CONV_TAIL, MIX_W), lambda b, t: (b, 0, 0)),
        pl.BlockSpec((None, CONV_W, MIX_W), lambda b, t: (layer, 0, 0)),
        pl.BlockSpec((None, 1, MIX_W), lambda b, t: (layer, 0, 0)),
        pl.BlockSpec((None, 1, MIX_W), lambda b, t: (layer, 0, 0)),
        pl.BlockSpec((None, 1, MIX_W), lambda b, t: (layer, 0, 0)),
    ]
    args = [proj, prev, lp["conv_dw"], lp["conv_dw_b"], lp["conv_ln_w"], lp["conv_ln_b"]]
    kern = functools.partial(_conv_kernel, nt=nt)
    aliases = {}
    if out_prev is not None:
        in_specs.append(pl.BlockSpec(memory_space=pl.ANY))
        args.append(out_prev)
        aliases = {len(args) - 1: 0}
        kern = _drop_alias_arg(kern, n_in=len(args) - 1)
    return pl.pallas_call(
        kern,
        grid=(batch, nt),
        in_specs=in_specs,
        out_specs=[pl.BlockSpec((tb, MIX_W), lambda b, t: (rb0 + b * nt + t, 0)),
                   pl.BlockSpec((None, CONV_W - 1, MIX_W), lambda b, t: (b, 0, 0))],
        out_shape=[jax.ShapeDtypeStruct((proj.shape[0], MIX_W), BF16),
                   jax.ShapeDtypeStruct((batch, CONV_W - 1, MIX_W), F32)],
        scratch_shapes=[pltpu.VMEM((tb + CONV_TAIL, MIX_W), F32)],
        input_output_aliases=aliases,
        compiler_params=_params(("parallel", "arbitrary")),
        name="conv",
    )(*args)


def _rel_bias_kernel(table_ref, o_ref):
    i = pl.program_id(1)
    j = lax.broadcasted_iota(jnp.int32, (REL_PAD, BAND), 1)
    m = lax.broadcasted_iota(jnp.int32, (REL_PAD, BAND), 0)
    idx = jnp.clip(BAND_PAST + i - j, -REL_CLIP, REL_CLIP) + REL_CLIP
    onehot = (idx == m).astype(F32)
    o_ref[...] = jnp.dot(table_ref[...], onehot, preferred_element_type=F32, precision=lax.Precision.HIGHEST)


def _rel_bias(table_pad):
    depth = table_pad.shape[0]
    return pl.pallas_call(
        _rel_bias_kernel,
        grid=(depth, CHUNK),
        in_specs=[pl.BlockSpec((None, HEADS, REL_PAD), lambda l, i: (l, 0, 0))],
        out_specs=pl.BlockSpec((None, None, HEADS, BAND), lambda l, i: (l, i, 0, 0)),
        out_shape=jax.ShapeDtypeStruct((depth, CHUNK, HEADS, BAND), F32),
        compiler_params=_params(("parallel", "parallel")),
        name="rel_bias",
    )(table_pad)


def _attend(q_h, k_h, v_h, bias_h, valid):
    s = _dot_nt(q_h, k_h) * (HEAD ** -0.5) + bias_h
    if valid is not None:
        s = jnp.where(valid, s, -1e30)
    e = jnp.exp(s - jnp.max(s, axis=-1, keepdims=True))
    pr = e / jnp.sum(e, axis=-1, keepdims=True)
    return _dot(pr.astype(BF16), v_h)


def _attn_prompt_kernel(q_ref, k_ref, v_ref, bias_ref, o_ref, kp_ref, vp_ref, oc_ref):
    seq = q_ref.shape[0]
    kp_ref[0:BAND_PAST, :] = jnp.zeros((BAND_PAST, MIX_W), BF16)
    vp_ref[0:BAND_PAST, :] = jnp.zeros((BAND_PAST, MIX_W), BF16)
    kp_ref[BAND_PAST:, :] = k_ref[...].astype(BF16)
    vp_ref[BAND_PAST:, :] = v_ref[...].astype(BF16)
    col = lax.broadcasted_iota(jnp.int32, (CHUNK, BAND), 1)

    def chunk(c, carry):
        r0 = pl.multiple_of(c * CHUNK, CHUNK)
        qc = q_ref[pl.ds(r0, CHUNK), :].astype(BF16)
        kb = kp_ref[pl.ds(r0, BAND), :]
        vb = vp_ref[pl.ds(r0, BAND), :]
        valid = (r0 + col) >= BAND_PAST
        for h in range(HEADS):
            hs = slice(h * HEAD, (h + 1) * HEAD)
            oc_ref[:, hs] = _attend(qc[:, hs], kb[:, hs], vb[:, hs], bias_ref[h], valid)
        o_ref[pl.ds(r0, CHUNK), :] = oc_ref[...].astype(BF16)
        return carry

    lax.fori_loop(0, seq // CHUNK, chunk, 0)


def _attn_prompt(proj, bias, layer, *, batch, seq):
    return pl.pallas_call(
        _attn_prompt_kernel,
        grid=(batch,),
        in_specs=[
            pl.BlockSpec((seq, MIX_W), lambda b: (b, COL_Q)),
            pl.BlockSpec((seq, MIX_W), lambda b: (b, COL_Q + 1)),
            pl.BlockSpec((seq, MIX_W), lambda b: (b, COL_Q + 2)),
            pl.BlockSpec((None, HEADS, CHUNK, BAND), lambda b: (layer, 0, 0, 0)),
        ],
        out_specs=pl.BlockSpec((seq, MIX_W), lambda b: (b, 0)),
        out_shape=jax.ShapeDtypeStruct((proj.shape[0], MIX_W), BF16),
        scratch_shapes=[pltpu.VMEM((BAND_PAST + seq, MIX_W), BF16), pltpu.VMEM((BAND_PAST + seq, MIX_W), BF16),
                        pltpu.VMEM((CHUNK, MIX_W), F32)],
        compiler_params=_params(("parallel",)),
        name="attn_prompt",
    )(proj, proj, proj, bias)


def _attn_sample_kernel(q_ref, k_ref, v_ref, ck_ref, cv_ref, bias_ref, o_ref, kp_ref, vp_ref, oc_ref):
    t = q_ref.shape[0]
    past = ck_ref.shape[0]
    kp_ref[0:past, :] = ck_ref[...].astype(BF16)
    vp_ref[0:past, :] = cv_ref[...].astype(BF16)
    kp_ref[past:, :] = k_ref[...].astype(BF16)
    vp_ref[past:, :] = v_ref[...].astype(BF16)
    qc = q_ref[...].astype(BF16)
    kb = kp_ref[...]
    vb = vp_ref[...]
    for h in range(HEADS):
        hs = slice(h * HEAD, (h + 1) * HEAD)
        oc_ref[:, hs] = _attend(qc[:, hs], kb[:, hs], vb[:, hs], bias_ref[h, :t, :past + t], None)
    o_ref[...] = oc_ref[...].astype(BF16)


def _attn_sample(proj, out_prev, cache_k, cache_v, bias, layer, *, row0, batch, seq):
    rb0 = row0 // seq
    past = cache_k.shape[2]
    kern = _drop_alias_arg(_attn_sample_kernel, n_in=6)
    return pl.pallas_call(
        kern,
        grid=(batch,),
        in_specs=[
            pl.BlockSpec((seq, MIX_W), lambda b: (rb0 + b, COL_Q)),
            pl.BlockSpec((seq, MIX_W), lambda b: (rb0 + b, COL_Q + 1)),
            pl.BlockSpec((seq, MIX_W), lambda b: (rb0 + b, COL_Q + 2)),
            pl.BlockSpec((None, None, past, MIX_W), lambda b: (layer, b, 0, 0)),
            pl.BlockSpec((None, None, past, MIX_W), lambda b: (layer, b, 0, 0)),
            pl.BlockSpec((None, HEADS, CHUNK, BAND), lambda b: (layer, 0, 0, 0)),
            pl.BlockSpec(memory_space=pl.ANY),
        ],
        out_specs=pl.BlockSpec((seq, MIX_W), lambda b: (rb0 + b, 0)),
        out_shape=jax.ShapeDtypeStruct((proj.shape[0], MIX_W), BF16),
        scratch_shapes=[pltpu.VMEM((past + seq, MIX_W), BF16), pltpu.VMEM((past + seq, MIX_W), BF16),
                        pltpu.VMEM((seq, MIX_W), F32)],
        input_output_aliases={6: 0},
        compiler_params=_params(("parallel",)),
        name="attn_sample",
    )(proj, proj, proj, cache_k, cache_v, bias, out_prev)


def _pad_rwkv_cols(x):
    o = 3 * MIX_W
    pad = [(0, 0)] * (x.ndim - 1) + [(0, LORA_PAD - W_RANK)]
    return jnp.concatenate([x[..., :o], jnp.pad(x[..., o:o + W_RANK], pad),
                            jnp.pad(x[..., o + W_RANK:o + W_RANK + A_RANK], pad),
                            x[..., o + W_RANK + A_RANK:]], axis=-1)


def _unpad_rwkv_cols(x):
    o = 3 * MIX_W
    return jnp.concatenate([x[..., :o], x[..., o:o + W_RANK], x[..., o + LORA_PAD:o + LORA_PAD + A_RANK],
                            x[..., o + 2 * LORA_PAD:]], axis=-1)


def _pad_rows(x, rows):
    return jnp.pad(x, [(0, 0)] * (x.ndim - 2) + [(0, rows - x.shape[-2]), (0, 0)])


def kernel(x_prompt, x_sample, state_rwkv_shift, state_rwkv_wkv, cache_conv, cache_attn_k, cache_attn_v,
           norm_mix, norm_ffn, norm_final, w_in, rwkv_mu, rwkv_w0, rwkv_w2, rwkv_a0, rwkv_a2, rwkv_g2,
           rwkv_kk, rwkv_ka, rwkv_rk, rwkv_gn_w, rwkv_gn_b, gmlp_ln_w, gmlp_ln_b, gmlp_ws, gmlp_bs,
           conv_dw, conv_dw_b, conv_ln_w, conv_ln_b, attn_rel_bias, w_branch, w_out, w_ffn_in, w_ffn_out):
    bp, seq, d = x_prompt.shape
    bs, dseq, _ = x_sample.shape
    depth = w_in.shape[0]
    rows_p = bp * seq
    rows_s = bs * dseq
    off_gate = RWKV_COLS + 7 * MIX_W
    assert seq % GMLP_CHUNK == 0 and dseq <= CHUNK and rows_p % dseq == 0

    w_mix = jnp.concatenate([_pad_rwkv_cols(w_in[..., :RWKV_COLS]), w_in[..., RWKV_COLS:off_gate]],
                            axis=-1).astype(BF16)
    w_gate = w_in[..., off_gate:].astype(BF16)
    w_branch_b = w_branch.astype(BF16)
    w_out_b = w_out.astype(BF16)
    w_ffn_in_b = w_ffn_in.astype(BF16)
    w_ffn_out_b = w_ffn_out.astype(BF16)
    row3 = lambda p: p.reshape(depth, 1, -1)
    lp = {
        "rwkv_mu": row3(_pad_rwkv_cols(rwkv_mu)), "rwkv_w0": row3(rwkv_w0), "rwkv_a0": row3(rwkv_a0),
        "rwkv_w2": _pad_rows(rwkv_w2, LORA_PAD).astype(BF16), "rwkv_a2": _pad_rows(rwkv_a2, LORA_PAD).astype(BF16),
        "rwkv_g2": rwkv_g2.astype(BF16), "rwkv_kk": row3(rwkv_kk), "rwkv_ka": row3(rwkv_ka),
        "rwkv_rk": row3(rwkv_rk), "rwkv_gn_w": row3(rwkv_gn_w), "rwkv_gn_b": row3(rwkv_gn_b),
        "ones_bd": jnp.kron(jnp.eye(HEADS, dtype=F32), jnp.ones((HEAD, HEAD), F32)).astype(BF16),
        "gmlp_ln_w": row3(gmlp_ln_w), "gmlp_ln_b": row3(gmlp_ln_b), "gmlp_ws": gmlp_ws,
        "gmlp_bias": jnp.repeat(jnp.swapaxes(gmlp_bs, 1, 2), HEAD, axis=2),
        "conv_dw": conv_dw, "conv_dw_b": row3(conv_dw_b), "conv_ln_w": row3(conv_ln_w),
        "conv_ln_b": row3(conv_ln_b),
    }
    g_mix = row3(norm_mix)
    g_ffn = row3(norm_ffn)
    table_pad = jnp.pad(attn_rel_bias, ((0, 0), (0, 0), (0, REL_PAD - attn_rel_bias.shape[-1])))
    bias = jnp.swapaxes(_rel_bias(table_pad), 1, 2)

    past = cache_attn_k.shape[2]
    cache_k = cache_attn_k.reshape(depth, bs, past, MIX_W)
    cache_v = cache_attn_v.reshape(depth, bs, past, MIX_W)
    shift_s = _pad_rwkv_cols(state_rwkv_shift).reshape(depth, bs, 1, RWKV_PAD_COLS)
    conv_s = jnp.pad(cache_conv, ((0, 0), (0, 0), (CONV_TAIL - (CONV_W - 1), 0), (0, 0)))
    shift_p = jnp.zeros((bp, 1, RWKV_PAD_COLS), F32)
    wkv_p = jnp.zeros((bp, HEADS, HEAD, HEAD), F32)
    conv_p = jnp.zeros((bp, CONV_TAIL, MIX_W), F32)

    x = jnp.concatenate([x_prompt.reshape(rows_p, d), x_sample.reshape(rows_s, d)], axis=0)
    keep = min(BAND_PAST, seq)
    conv_tb = _pick_tile(seq, 256, 32)
    outs = {k: [] for k in ("p_shift", "p_wkv", "p_conv", "p_k", "p_v",
                            "s_shift", "s_wkv", "s_conv", "s_k", "s_v", "s_gv")}
    for l in range(depth):
        proj, h = _inproj(x, g_mix, w_mix, l)

        o_rwkv, wkv_new_p = _rwkv(proj, None, shift_p, wkv_p, lp, l, row0=0, batch=bp, seq=seq, chunk=CHUNK)
        o_rwkv, wkv_new_s = _rwkv(proj, o_rwkv, shift_s[l], state_rwkv_wkv[l], lp, l,
                                  row0=rows_p, batch=bs, seq=dseq, chunk=dseq)
        (o_gmlp,) = _gmlp(proj, None, lp, l, row0=0, batch=bp, seq=seq, chunk=GMLP_CHUNK, emit_v=False)
        o_gmlp, gv_s = _gmlp(proj, o_gmlp, lp, l, row0=rows_p, batch=bs, seq=dseq, chunk=dseq, emit_v=True)
        o_conv, conv_new_p = _conv(proj, None, conv_p, lp, l, row0=0, batch=bp, seq=seq, tb=conv_tb)
        o_conv, conv_new_s = _conv(proj, o_conv, conv_s[l], lp, l, row0=rows_p, batch=bs, seq=dseq, tb=dseq)
        o_attn = _attn_prompt(proj, bias, l, batch=bp, seq=seq)
        o_attn = _attn_sample(proj, o_attn, cache_k, cache_v, bias, l, row0=rows_p, batch=bs, seq=dseq)

        merged = _merge(h, w_gate, (o_rwkv, o_gmlp, o_conv, o_attn), w_branch_b, l)
        x = _resid(merged, w_out_b, x, l)
        act = _ffn_in(x, g_ffn, w_ffn_in_b, l)
        x = _resid(act, w_ffn_out_b, x, l)

        pp = proj[:rows_p].reshape(bp, seq, MIX_COLS)
        ps = proj[rows_p:].reshape(bs, dseq, MIX_COLS)
        kcol = COL_Q * MIX_W + MIX_W
        outs["p_shift"].append(_unpad_rwkv_cols(pp[:, -1, :RWKV_PAD_COLS]))
        outs["p_wkv"].append(wkv_new_p)
        outs["p_conv"].append(conv_new_p)
        outs["p_k"].append(pp[:, seq - keep:, kcol:kcol + MIX_W].reshape(bp, keep, HEADS, HEAD))
        outs["p_v"].append(pp[:, seq - keep:, kcol + MIX_W:].reshape(bp, keep, HEADS, HEAD))
        outs["s_shift"].append(_unpad_rwkv_cols(ps[:, -1, :RWKV_PAD_COLS]))
        outs["s_wkv"].append(wkv_new_s)
        outs["s_conv"].append(conv_new_s)
        outs["s_k"].append(ps[:, :, kcol:kcol + MIX_W].reshape(bs, dseq, HEADS, HEAD))
        outs["s_v"].append(ps[:, :, kcol + MIX_W:].reshape(bs, dseq, HEADS, HEAD))
        outs["s_gv"].append(gv_s.reshape(bs, dseq, MIX_W))

    y = _final_norm(x, norm_final.reshape(1, d))
    st = {k: jnp.stack(v) for k, v in outs.items()}
    return (y[:rows_p].reshape(bp, seq, d), y[rows_p:].reshape(bs, dseq, d),
            st["p_shift"], st["p_wkv"], st["p_conv"], st["p_k"], st["p_v"],
            st["s_shift"], st["s_wkv"], st["s_conv"], st["s_k"], st["s_v"], st["s_gv"])
```

```python
import functools

import jax
import jax.numpy as jnp
from jax import lax
from jax.experimental import pallas as pl
from jax.experimental.pallas import tpu as pltpu

F32 = jnp.float32
BF16 = jnp.bfloat16

MIX_W = 512
N_BRANCH = 4
HEADS = 8
HEAD = 64
W_RANK = 96
A_RANK = 96
G_RANK = 256
LORA_PAD = 128
RWKV_COLS = 3 * MIX_W + W_RANK + A_RANK + G_RANK
RWKV_PAD_COLS = 3 * MIX_W + 2 * LORA_PAD + G_RANK
GN_EPS = 64e-5
RMS_EPS = 1e-6
LN_EPS = 1e-5
CHUNK = 64
GMLP_CHUNK = 128
CONV_W = 31
CONV_TAIL = 32
BAND_PAST = 8 * CHUNK
BAND = BAND_PAST + CHUNK
REL_CLIP = 128
REL_PAD = 384
MIX_COLS = RWKV_PAD_COLS + 2 * MIX_W + 2 * MIX_W + 3 * MIX_W
COL_GMLP = RWKV_PAD_COLS // (2 * MIX_W)
COL_CONV = COL_GMLP + 1
COL_Q = (RWKV_PAD_COLS + 4 * MIX_W) // MIX_W

VMEM_LIMIT_BYTES = 52 * 1024 * 1024
ROW_TILE_SMALL = 768
ROW_TILE_LARGE = 1408
COL_TILE_TARGET = 512
RWKV_STEP_ROWS = 2 * CHUNK


def _pick_tile(n, target, mult):
    best = None
    for t in range(mult, min(n, target) + 1, mult):
        if n % t == 0:
            best = t
    assert best is not None, (n, target, mult)
    return best


def _params(sem):
    return pltpu.CompilerParams(dimension_semantics=sem, vmem_limit_bytes=VMEM_LIMIT_BYTES)


def _rms_bf16(x, g):
    ms = jnp.mean(x * x, axis=-1, keepdims=True)
    return (x * lax.rsqrt(ms + RMS_EPS) * g).astype(BF16)


def _layer_norm(x, w, b):
    mu = jnp.mean(x, axis=-1, keepdims=True)
    var = jnp.mean(jnp.square(x - mu), axis=-1, keepdims=True)
    return (x - mu) * lax.rsqrt(var + LN_EPS) * w + b


def _gelu_tanh(x):
    return 0.5 * x * (1.0 + jnp.tanh(0.7978845608028654 * (x + 0.044715 * x * x * x)))


def _dot(a, b):
    return jnp.dot(a, b, preferred_element_type=F32)


def _dot_nt(a, b):
    return lax.dot_general(a, b, (((1,), (1,)), ((), ())), preferred_element_type=F32)


def _dot_tn(a, b):
    return lax.dot_general(a, b, (((0,), (0,)), ((), ())), preferred_element_type=F32)


def _inproj_kernel(x_ref, g_ref, w_ref, proj_ref, h_ref):
    @pl.when(pl.program_id(1) == 0)
    def _():
        h_ref[...] = _rms_bf16(x_ref[...], g_ref[...])

    proj_ref[...] = _dot(h_ref[...], w_ref[...])


def _inproj(x, g, w_mix, layer):
    m, d = x.shape
    n = w_mix.shape[-1]
    tm = _pick_tile(m, ROW_TILE_LARGE, 16)
    tn = _pick_tile(n, COL_TILE_TARGET, 128)
    return pl.pallas_call(
        _inproj_kernel,
        grid=(m // tm, n // tn),
        in_specs=[
            pl.BlockSpec((tm, d), lambda i, j: (i, 0)),
            pl.BlockSpec((None, 1, d), lambda i, j: (layer, 0, 0)),
            pl.BlockSpec((None, d, tn), lambda i, j: (layer, 0, j)),
        ],
        out_specs=[
            pl.BlockSpec((tm, tn), lambda i, j: (i, j)),
            pl.BlockSpec((tm, d), lambda i, j: (i, 0)),
        ],
        out_shape=[jax.ShapeDtypeStruct((m, n), F32), jax.ShapeDtypeStruct((m, d), BF16)],
        compiler_params=_params(("parallel", "arbitrary")),
        name="inproj",
    )(x, g, w_mix)


def _merge_kernel(h_ref, wg0, wg1, wg2, wg3, b0, b1, b2, b3, wb_ref, o_ref):
    h = h_ref[...]
    acc = None
    for n, (wg, br) in enumerate(((wg0, b0), (wg1, b1), (wg2, b2), (wg3, b3))):
        gate = jax.nn.sigmoid(_dot(h, wg[...]))
        term = gate * _dot(br[...], wb_ref[n])
        acc = term if acc is None else acc + term
    o_ref[...] = acc.astype(BF16)


def _merge(h, w_gate, branches, w_branch, layer):
    m, d = h.shape
    tm = _pick_tile(m, ROW_TILE_SMALL, 16)
    tn = _pick_tile(d, COL_TILE_TARGET, 128)
    nj = d // tn
    gate_specs = [pl.BlockSpec((None, d, tn), functools.partial(lambda i, j, n: (layer, 0, n * nj + j), n=n))
                  for n in range(N_BRANCH)]
    br_specs = [pl.BlockSpec((tm, MIX_W), lambda i, j: (i, 0)) for _ in range(N_BRANCH)]
    return pl.pallas_call(
        _merge_kernel,
        grid=(m // tm, nj),
        in_specs=[pl.BlockSpec((tm, d), lambda i, j: (i, 0))] + gate_specs + br_specs
        + [pl.BlockSpec((None, N_BRANCH, MIX_W, tn), lambda i, j: (layer, 0, 0, j))],
        out_specs=pl.BlockSpec((tm, tn), lambda i, j: (i, j)),
        out_shape=jax.ShapeDtypeStruct((m, d), BF16),
        compiler_params=_params(("parallel", "arbitrary")),
        name="merge",
    )(h, w_gate, w_gate, w_gate, w_gate, *branches, w_branch)


def _resid_kernel(a_ref, w_ref, x_ref, o_ref):
    o_ref[...] = x_ref[...] + _dot(a_ref[...], w_ref[...])


def _resid(a, w, x, layer):
    m, k = a.shape
    d = x.shape[-1]
    tm = _pick_tile(m, ROW_TILE_LARGE if 8 * ROW_TILE_LARGE * k * 2 <= VMEM_LIMIT_BYTES else ROW_TILE_SMALL, 16)
    tn = _pick_tile(d, COL_TILE_TARGET, 128)
    return pl.pallas_call(
        _resid_kernel,
        grid=(m // tm, d // tn),
        in_specs=[
            pl.BlockSpec((tm, k), lambda i, j: (i, 0)),
            pl.BlockSpec((None, k, tn), lambda i, j: (layer, 0, j)),
            pl.BlockSpec((tm, tn), lambda i, j: (i, j)),
        ],
        out_specs=pl.BlockSpec((tm, tn), lambda i, j: (i, j)),
        out_shape=jax.ShapeDtypeStruct((m, d), F32),
        compiler_params=_params(("parallel", "arbitrary")),
        name="resid",
    )(a, w, x)


def _ffn_in_kernel(x_ref, g_ref, wg_ref, wu_ref, act_ref, h_scr):
    @pl.when(pl.program_id(1) == 0)
    def _():
        h_scr[...] = _rms_bf16(x_ref[...], g_ref[...])

    h = h_scr[...]
    act_ref[...] = (jax.nn.silu(_dot(h, wg_ref[...])) * _dot(h, wu_ref[...])).astype(BF16)


def _ffn_in(x, g, w_ffn_in, layer):
    m, d = x.shape
    dff = w_ffn_in.shape[-1] // 2
    tm = _pick_tile(m, ROW_TILE_LARGE, 16)
    tn = _pick_tile(dff, COL_TILE_TARGET, 128)
    nj = dff // tn
    return pl.pallas_call(
        _ffn_in_kernel,
        grid=(m // tm, nj),
        in_specs=[
            pl.BlockSpec((tm, d), lambda i, j: (i, 0)),
            pl.BlockSpec((None, 1, d), lambda i, j: (layer, 0, 0)),
            pl.BlockSpec((None, d, tn), lambda i, j: (layer, 0, j)),
            pl.BlockSpec((None, d, tn), lambda i, j: (layer, 0, nj + j)),
        ],
        out_specs=pl.BlockSpec((tm, tn), lambda i, j: (i, j)),
        out_shape=jax.ShapeDtypeStruct((m, dff), BF16),
        scratch_shapes=[pltpu.VMEM((tm, d), BF16)],
        compiler_params=_params(("parallel", "arbitrary")),
        name="ffn_in",
    )(x, g, w_ffn_in, w_ffn_in)


def _final_norm_kernel(x_ref, g_ref, o_ref):
    x = x_ref[...]
    ms = jnp.mean(x * x, axis=-1, keepdims=True)
    o_ref[...] = x * lax.rsqrt(ms + RMS_EPS) * g_ref[...]


def _final_norm(x, g, *, row0, rows):
    d = x.shape[-1]
    tm = _pick_tile(rows, ROW_TILE_SMALL, 16)
    assert row0 % tm == 0
    rb0 = row0 // tm
    return pl.pallas_call(
        _final_norm_kernel,
        grid=(rows // tm,),
        in_specs=[pl.BlockSpec((tm, d), lambda i: (rb0 + i, 0)), pl.BlockSpec((1, d), lambda i: (0, 0))],
        out_specs=pl.BlockSpec((tm, d), lambda i: (i, 0)),
        out_shape=jax.ShapeDtypeStruct((rows, d), F32),
        compiler_params=_params(("parallel",)),
        name="final_norm",
    )(x, g)


def _split3(x):
    hi = x.astype(BF16)
    r1 = x - hi.astype(F32)
    mid = r1.astype(BF16)
    lo = (r1 - mid.astype(F32)).astype(BF16)
    return hi, mid, lo


def _head_sum(x, ones_bd):
    rows = x.shape[0]
    s = _dot(jnp.concatenate(_split3(x), axis=0), ones_bd)
    return s[:rows] + s[rows:2 * rows] + s[2 * rows:]


def _rwkv_kernel(p_ref, shift0_ref, s0_ref, mu_ref, w0_ref, w2_ref, a0_ref, a2_ref, g2_ref, kkw_ref, ka_ref,
                 rk_ref, gnw_ref, gnb_ref, ones_ref, tri_ref, o_ref, sout_ref, shift_out_ref,
                 carry_ref, s_ref, y_ref, *, nc, L):
    rows = p_ref.shape[0]
    ncb = rows // L
    c = pl.program_id(1)

    @pl.when(c == 0)
    def _():
        carry_ref[...] = shift0_ref[...]
        s_ref[...] = s0_ref[...]

    p = p_ref[...]
    row = lax.broadcasted_iota(jnp.int32, p.shape, 0)
    prev = jnp.where(row == 0, carry_ref[...], pltpu.roll(p, 1, 0))
    carry_ref[...] = p[rows - 1:rows, :]
    xs = p + mu_ref[...] * (prev - p)

    o_w = 3 * MIX_W
    r = xs[:, 0:MIX_W]
    k = xs[:, MIX_W:2 * MIX_W]
    v = xs[:, 2 * MIX_W:3 * MIX_W]
    wl = xs[:, o_w:o_w + LORA_PAD]
    al = xs[:, o_w + LORA_PAD:o_w + 2 * LORA_PAD]
    gl = xs[:, o_w + 2 * LORA_PAD:]
    ones_bd = ones_ref[...]

    zz = w0_ref[...] + _dot(jnp.tanh(wl).astype(BF16), w2_ref[...])
    nz = -zz
    softplus = jnp.maximum(nz, 0.0) + jnp.log(1.0 + jnp.exp(-jnp.abs(nz)))
    lw = -jnp.exp(-softplus - 0.5)
    a = jax.nn.sigmoid(a0_ref[...] + _dot(al.astype(BF16), a2_ref[...]))
    g = _dot(jax.nn.sigmoid(gl).astype(BF16), g2_ref[...])
    kk = k * kkw_ref[...]
    kk = kk * lax.rsqrt(jnp.maximum(_head_sum(kk * kk, ones_bd), 1e-24))
    k = k * (1.0 + (a - 1.0) * ka_ref[...])
    a_s = -kk
    b_s = kk * a
    bonus = _head_sum(r * k * rk_ref[...], ones_bd) * v

    cum3 = _dot(tri_ref[...], jnp.concatenate(_split3(lw), axis=1))
    cum = cum3[:, :MIX_W] + cum3[:, MIX_W:2 * MIX_W] + cum3[:, 2 * MIX_W:]
    ends = [cum[(ch + 1) * L - 1:(ch + 1) * L, :] for ch in range(ncb)]
    cl = jnp.concatenate([jnp.broadcast_to(e, (L, MIX_W)) for e in ends], axis=0) if ncb > 1 else ends[0]
    inv_p = jnp.exp(-cum)
    to_end = jnp.exp(cl - cum)
    at_f = a_s * jnp.exp(cum - lw)
    at = at_f.astype(BF16)
    bt = (b_s * inv_p).astype(BF16)
    kt = (k * inv_p).astype(BF16)
    rt = (r * jnp.exp(cum)).astype(BF16)
    bh = (b_s * to_end).astype(BF16)
    kh = (k * to_end).astype(BF16)
    p_end = [jnp.exp(e) for e in ends]
    vb = v.astype(BF16)

    ri = lax.broadcasted_iota(jnp.int32, (L, 2 * L), 0)
    ci = lax.broadcasted_iota(jnp.int32, (L, 2 * L), 1)
    ci = jnp.where(ci >= L, ci - L, ci)
    strict = ci < ri
    incl = ci <= ri

    chains = [(ch, h) for ch in range(ncb) for h in range(HEADS)]
    sl = lambda arr, ch, h: arr[ch * L:(ch + 1) * L, h * HEAD:(h + 1) * HEAD]
    gm = [_dot_nt(jnp.concatenate([sl(at, ch, h), sl(rt, ch, h)], axis=0),
                  jnp.concatenate([sl(kt, ch, h), sl(bt, ch, h)], axis=0)) for ch, h in chains]
    a_top = [jnp.where(strict, m[:L], 0.0) for m in gm]
    m_low = [jnp.where(incl, m[L:], 0.0).astype(BF16) for m in gm]
    q = [t[:, L:] for t in a_top]
    x = [jnp.concatenate([t[:, :L], sl(at_f, ch, h)], axis=1) for t, (ch, h) in zip(a_top, chains)]
    n_stage = L.bit_length() - 1
    for i in range(n_stage):
        last = i == n_stage - 1
        qb = [t.astype(BF16) for t in q]
        rhs = [t.astype(BF16) for t in x] if last else \
              [jnp.concatenate([t.astype(BF16), u], axis=1) for t, u in zip(x, qb)]
        res = [_dot(u, t) for u, t in zip(qb, rhs)]
        x = [t + u[:, :L + HEAD] for t, u in zip(x, res)]
        if not last:
            q = [u[:, L + HEAD:] for u in res]
    u_free = [_dot(t[:, :L].astype(BF16), sl(vb, ch, h)) for t, (ch, h) in zip(x, chains)]
    wm_rt = [jnp.concatenate([t[:, L:].astype(BF16), sl(rt, ch, h)], axis=0) for t, (ch, h) in zip(x, chains)]
    kb = [jnp.concatenate([sl(kh, ch, h), sl(bh, ch, h)], axis=0) for ch, h in chains]

    state = [s_ref[h] for h in range(HEADS)]
    for ch in range(ncb):
        ids = [ch * HEADS + h for h in range(HEADS)]
        ws = [_dot_nt(wm_rt[i], state[h].astype(BF16)) for h, i in enumerate(ids)]
        uv = [jnp.concatenate([sl(vb, ch, h), (ws[h][:L] + u_free[i]).astype(BF16)], axis=0)
              for h, i in enumerate(ids)]
        for h, i in enumerate(ids):
            y_ref[ch * L:(ch + 1) * L, h * HEAD:(h + 1) * HEAD] = ws[h][L:] + _dot(m_low[i], uv[h])
        state = [state[h] * p_end[ch][:, h * HEAD:(h + 1) * HEAD] + _dot_tn(uv[h], kb[i])
                 for h, i in enumerate(ids)]
    for h in range(HEADS):
        s_ref[h] = state[h]

    y = y_ref[...]
    mu_y = _head_sum(y, ones_bd) * (1.0 / HEAD)
    yc = y - mu_y
    var_y = _head_sum(yc * yc, ones_bd) * (1.0 / HEAD)
    yn = yc * lax.rsqrt(var_y + GN_EPS) * gnw_ref[...] + gnb_ref[...]
    o_ref[...] = ((yn + bonus) * g).astype(BF16)

    @pl.when(c == nc - 1)
    def _():
        sout_ref[...] = s_ref[...]
        shift_out_ref[...] = p[rows - 1:rows, :]


def _rwkv(proj, out_prev, shift0, s0, lp, layer, *, row0, batch, seq, chunk, rows):
    nc = seq // rows
    rb0 = row0 // rows
    vec = lambda name: pl.BlockSpec((None, 1, lp[name].shape[-1]), lambda b, c: (layer, 0, 0))
    mat = lambda name: pl.BlockSpec((None,) + lp[name].shape[1:], lambda b, c: (layer, 0, 0))
    const = lambda arr: pl.BlockSpec(arr.shape, lambda b, c: (0,) * arr.ndim)
    tri = jnp.kron(jnp.eye(rows // chunk, dtype=F32), jnp.tril(jnp.ones((chunk, chunk), F32))).astype(BF16)
    ones_bd = lp["ones_bd"]
    in_specs = [
        pl.BlockSpec((rows, RWKV_PAD_COLS), lambda b, c: (rb0 + b * nc + c, 0)),
        pl.BlockSpec((None, 1, RWKV_PAD_COLS), lambda b, c: (b, 0, 0)),
        pl.BlockSpec((None, HEADS, HEAD, HEAD), lambda b, c: (b, 0, 0, 0)),
        vec("rwkv_mu"), vec("rwkv_w0"), mat("rwkv_w2"), vec("rwkv_a0"), mat("rwkv_a2"), mat("rwkv_g2"),
        vec("rwkv_kk"), vec("rwkv_ka"), vec("rwkv_rk"), vec("rwkv_gn_w"), vec("rwkv_gn_b"),
        const(ones_bd), const(tri),
    ]
    args = [proj, shift0, s0, lp["rwkv_mu"], lp["rwkv_w0"], lp["rwkv_w2"], lp["rwkv_a0"], lp["rwkv_a2"],
            lp["rwkv_g2"], lp["rwkv_kk"], lp["rwkv_ka"], lp["rwkv_rk"], lp["rwkv_gn_w"], lp["rwkv_gn_b"],
            ones_bd, tri]
    kern = functools.partial(_rwkv_kernel, nc=nc, L=chunk)
    aliases = {}
    if out_prev is not None:
        in_specs.append(pl.BlockSpec(memory_space=pl.ANY))
        args.append(out_prev)
        aliases = {len(args) - 1: 0}
        kern = _drop_alias_arg(kern, n_in=len(args) - 1)
    return pl.pallas_call(
        kern,
        grid=(batch, nc),
        in_specs=in_specs,
        out_specs=[pl.BlockSpec((rows, MIX_W), lambda b, c: (rb0 + b * nc + c, 0)),
                   pl.BlockSpec((None, HEADS, HEAD, HEAD), lambda b, c: (b, 0, 0, 0)),
                   pl.BlockSpec((None, 1, RWKV_PAD_COLS), lambda b, c: (b, 0, 0))],
        out_shape=[jax.ShapeDtypeStruct((proj.shape[0], MIX_W), BF16),
                   jax.ShapeDtypeStruct((batch, HEADS, HEAD, HEAD), F32),
                   jax.ShapeDtypeStruct((batch, 1, RWKV_PAD_COLS), F32)],
        scratch_shapes=[pltpu.VMEM((1, RWKV_PAD_COLS), F32), pltpu.VMEM((HEADS, HEAD, HEAD), F32),
                        pltpu.VMEM((rows, MIX_W), F32)],
        input_output_aliases=aliases,
        compiler_params=_params(("parallel", "arbitrary")),
        name="rwkv",
    )(*args)


def _drop_alias_arg(kern, n_in):
    def wrapped(*refs):
        return kern(*refs[:n_in], *refs[n_in + 1:])
    return wrapped


def _gmlp_kernel(p_ref, lnw_ref, lnb_ref, ws_ref, bias_ref, o_ref, *rest, emit_v):
    if emit_v:
        v_ref, m_ref = rest
    else:
        (m_ref,) = rest
    L = p_ref.shape[0]
    p = p_ref[...]
    u = _gelu_tanh(p[:, :MIX_W])
    v = _layer_norm(_gelu_tanh(p[:, MIX_W:]), lnw_ref[...], lnb_ref[...])
    if emit_v:
        v_ref[...] = v
    vb = v.astype(BF16)
    lower = lax.broadcasted_iota(jnp.int32, (L, L), 1) <= lax.broadcasted_iota(jnp.int32, (L, L), 0)
    for g in range(HEADS):
        gs = slice(g * HEAD, (g + 1) * HEAD)
        w = jnp.where(lower, ws_ref[g, :L, :L], 0.0).astype(BF16)
        m_ref[:, gs] = _dot(w, vb[:, gs])
    o_ref[...] = (u * (m_ref[...] + bias_ref[...])).astype(BF16)


def _gmlp(proj, out_prev, lp, layer, *, row0, batch, seq, chunk, emit_v):
    nc = seq // chunk
    rb0 = row0 // chunk
    n_rows = batch * seq
    bias = lp["gmlp_bias"][:, :chunk, :]
    in_specs = [
        pl.BlockSpec((chunk, 2 * MIX_W), lambda i: (rb0 + i, COL_GMLP)),
        pl.BlockSpec((None, 1, MIX_W), lambda i: (layer, 0, 0)),
        pl.BlockSpec((None, 1, MIX_W), lambda i: (layer, 0, 0)),
        pl.BlockSpec((None, HEADS, GMLP_CHUNK, GMLP_CHUNK), lambda i: (layer, 0, 0, 0)),
        pl.BlockSpec((None, chunk, MIX_W), lambda i: (layer, 0, 0)),
    ]
    args = [proj, lp["gmlp_ln_w"], lp["gmlp_ln_b"], lp["gmlp_ws"], bias]
    out_specs = [pl.BlockSpec((chunk, MIX_W), lambda i: (rb0 + i, 0))]
    out_shape = [jax.ShapeDtypeStruct((proj.shape[0], MIX_W), BF16)]
    if emit_v:
        out_specs.append(pl.BlockSpec((chunk, MIX_W), lambda i: (i, 0)))
        out_shape.append(jax.ShapeDtypeStruct((n_rows, MIX_W), F32))
    kern = functools.partial(_gmlp_kernel, emit_v=emit_v)
    aliases = {}
    if out_prev is not None:
        in_specs.append(pl.BlockSpec(memory_space=pl.ANY))
        args.append(out_prev)
        aliases = {len(args) - 1: 0}
        kern = _drop_alias_arg(kern, n_in=len(args) - 1)
    return pl.pallas_call(
        kern,
        grid=(batch * nc,),
        in_specs=in_specs,
        out_specs=out_specs,
        out_shape=out_shape,
        scratch_shapes=[pltpu.VMEM((chunk, MIX_W), F32)],
        input_output_aliases=aliases,
        compiler_params=_params(("parallel",)),
        name="gmlp",
    )(*args)


def _conv_kernel(p_ref, prev_ref, dw_ref, dwb_ref, lnw_ref, lnb_ref, o_ref, state_ref, buf_ref, *, nt):
    tb = p_ref.shape[0]
    t = pl.program_id(1)

    @pl.when(t == 0)
    def _():
        buf_ref[0:CONV_TAIL, :] = prev_ref[...]

    @pl.when(t > 0)
    def _():
        buf_ref[0:CONV_TAIL, :] = buf_ref[tb:tb + CONV_TAIL, :]

    p = p_ref[...]
    buf_ref[CONV_TAIL:CONV_TAIL + tb, :] = p[:, :MIX_W] * jax.nn.sigmoid(p[:, MIX_W:])
    sub = min(tb, 32)
    off = CONV_TAIL - (CONV_W - 1)
    for r0 in range(0, tb, sub):
        acc = None
        for w in range(CONV_W):
            term = buf_ref[r0 + w + off:r0 + w + off + sub, :] * dw_ref[w:w + 1, :]
            acc = term if acc is None else acc + term
        y = _layer_norm(acc + dwb_ref[...], lnw_ref[...], lnb_ref[...])
        o_ref[r0:r0 + sub, :] = jax.nn.silu(y).astype(BF16)

    @pl.when(t == nt - 1)
    def _():
        state_ref[...] = buf_ref[tb + off:tb + CONV_TAIL, :]


def _conv(proj, out_prev, prev, lp, layer, *, row0, batch, seq, tb):
    nt = seq // tb
    rb0 = row0 // tb
    in_specs = [
        pl.BlockSpec((tb, 2 * MIX_W), lambda b, t: (rb0 + b * nt + t, COL_CONV)),
        pl.BlockSpec((None, CONV_TAIL, MIX_W), lambda b, t: (b, 0, 0)),
        pl.BlockSpec((None, CONV_W, MIX_W), lambda b, t: (layer, 0, 0)),
        pl.BlockSpec((None, 1, MIX_W), lambda b, t: (layer, 0, 0)),
        pl.BlockSpec((None, 1, MIX_W), lambda b, t: (layer, 0, 0)),
        pl.BlockSpec((None, 1, MIX_W), lambda b, t: (layer, 0, 0)),
    ]
    args = [proj, prev, lp["conv_dw"], lp["conv_dw_b"], lp["conv_ln_w"], lp["conv_ln_b"]]
    kern = functools.partial(_conv_kernel, nt=nt)
    aliases = {}
    if out_prev is not None:
        in_specs.append(pl.BlockSpec(memory_space=pl.ANY))
        args.append(out_prev)
        aliases = {len(args) - 1: 0}
        kern = _drop_alias_arg(kern, n_in=len(args) - 1)
    return pl.pallas_call(
        kern,
        grid=(batch, nt),
        in_specs=in_specs,
        out_specs=[pl.BlockSpec((tb, MIX_W), lambda b, t: (rb0 + b * nt + t, 0)),
                   pl.BlockSpec((None, CONV_W - 1, MIX_W), lambda b, t: (b, 0, 0))],
        out_shape=[jax.ShapeDtypeStruct((proj.shape[0], MIX_W), BF16),
                   jax.ShapeDtypeStruct((batch, CONV_W - 1, MIX_W), F32)],
        scratch_shapes=[pltpu.VMEM((tb + CONV_TAIL, MIX_W), F32)],
        input_output_aliases=aliases,
        compiler_params=_params(("parallel", "arbitrary")),
        name="conv",
    )(*args)


def _rel_bias_kernel(table_ref, o_ref):
    i = pl.program_id(1)
    j = lax.broadcasted_iota(jnp.int32, (REL_PAD, BAND), 1)
    m = lax.broadcasted_iota(jnp.int32, (REL_PAD, BAND), 0)
    idx = jnp.clip(BAND_PAST + i - j, -REL_CLIP, REL_CLIP) + REL_CLIP
    onehot = (idx == m).astype(F32)
    o_ref[...] = jnp.dot(table_ref[...], onehot, preferred_element_type=F32, precision=lax.Precision.HIGHEST)


def _rel_bias(table_pad):
    depth = table_pad.shape[0]
    return pl.pallas_call(
        _rel_bias_kernel,
        grid=(depth, CHUNK),
        in_specs=[pl.BlockSpec((None, HEADS, REL_PAD), lambda l, i: (l, 0, 0))],
        out_specs=pl.BlockSpec((None, None, HEADS, BAND), lambda l, i: (l, i, 0, 0)),
        out_shape=jax.ShapeDtypeStruct((depth, CHUNK, HEADS, BAND), F32),
        compiler_params=_params(("parallel", "parallel")),
        name="rel_bias",
    )(table_pad)


def _attend_heads(qc, kb, vb, bias_of, valid, oc_ref):
    heads = [slice(h * HEAD, (h + 1) * HEAD) for h in range(HEADS)]
    s = [_dot_nt(qc[:, hs], kb[:, hs]) * (HEAD ** -0.5) + bias_of(h) for h, hs in enumerate(heads)]
    if valid is not None:
        s = [jnp.where(valid, t, -1e30) for t in s]
    e = [jnp.exp(t - jnp.max(t, axis=-1, keepdims=True)) for t in s]
    pr = [(t / jnp.sum(t, axis=-1, keepdims=True)).astype(BF16) for t in e]
    for hs, t in zip(heads, pr):
        oc_ref[:, hs] = _dot(t, vb[:, hs])


def _attn_prompt_kernel(q_ref, k_ref, v_ref, bias_ref, o_ref, knew_ref, vnew_ref, kp_ref, vp_ref, oc_ref):
    seq = q_ref.shape[0]
    keep = knew_ref.shape[0]
    kp_ref[0:BAND_PAST, :] = jnp.zeros((BAND_PAST, MIX_W), BF16)
    vp_ref[0:BAND_PAST, :] = jnp.zeros((BAND_PAST, MIX_W), BF16)
    kp_ref[BAND_PAST:, :] = k_ref[...].astype(BF16)
    vp_ref[BAND_PAST:, :] = v_ref[...].astype(BF16)
    knew_ref[...] = k_ref[seq - keep:, :]
    vnew_ref[...] = v_ref[seq - keep:, :]
    col = lax.broadcasted_iota(jnp.int32, (CHUNK, BAND), 1)

    def chunk(c, carry):
        r0 = pl.multiple_of(c * CHUNK, CHUNK)
        qc = q_ref[pl.ds(r0, CHUNK), :].astype(BF16)
        kb = kp_ref[pl.ds(r0, BAND), :]
        vb = vp_ref[pl.ds(r0, BAND), :]
        _attend_heads(qc, kb, vb, lambda h: bias_ref[h], (r0 + col) >= BAND_PAST, oc_ref)
        o_ref[pl.ds(r0, CHUNK), :] = oc_ref[...].astype(BF16)
        return carry

    lax.fori_loop(0, seq // CHUNK, chunk, 0)


def _attn_prompt(proj, bias, layer, *, batch, seq, keep):
    return pl.pallas_call(
        _attn_prompt_kernel,
        grid=(batch,),
        in_specs=[
            pl.BlockSpec((seq, MIX_W), lambda b: (b, COL_Q)),
            pl.BlockSpec((seq, MIX_W), lambda b: (b, COL_Q + 1)),
            pl.BlockSpec((seq, MIX_W), lambda b: (b, COL_Q + 2)),
            pl.BlockSpec((None, HEADS, CHUNK, BAND), lambda b: (layer, 0, 0, 0)),
        ],
        out_specs=[pl.BlockSpec((seq, MIX_W), lambda b: (b, 0)),
                   pl.BlockSpec((None, keep, MIX_W), lambda b: (b, 0, 0)),
                   pl.BlockSpec((None, keep, MIX_W), lambda b: (b, 0, 0))],
        out_shape=[jax.ShapeDtypeStruct((proj.shape[0], MIX_W), BF16),
                   jax.ShapeDtypeStruct((batch, keep, MIX_W), F32),
                   jax.ShapeDtypeStruct((batch, keep, MIX_W), F32)],
        scratch_shapes=[pltpu.VMEM((BAND_PAST + seq, MIX_W), BF16), pltpu.VMEM((BAND_PAST + seq, MIX_W), BF16),
                        pltpu.VMEM((CHUNK, MIX_W), F32)],
        compiler_params=_params(("parallel",)),
        name="attn_prompt",
    )(proj, proj, proj, bias)


def _attn_sample_kernel(q_ref, k_ref, v_ref, ck_ref, cv_ref, bias_ref, o_ref, knew_ref, vnew_ref,
                        kp_ref, vp_ref, oc_ref):
    t = q_ref.shape[0]
    past = ck_ref.shape[0]
    kp_ref[0:past, :] = ck_ref[...].astype(BF16)
    vp_ref[0:past, :] = cv_ref[...].astype(BF16)
    kp_ref[past:, :] = k_ref[...].astype(BF16)
    vp_ref[past:, :] = v_ref[...].astype(BF16)
    knew_ref[...] = k_ref[...]
    vnew_ref[...] = v_ref[...]
    _attend_heads(q_ref[...].astype(BF16), kp_ref[...], vp_ref[...], lambda h: bias_ref[h, :t, :past + t], None,
                  oc_ref)
    o_ref[...] = oc_ref[...].astype(BF16)


def _attn_sample(proj, out_prev, cache_k, cache_v, bias, layer, *, row0, batch, seq):
    rb0 = row0 // seq
    past = cache_k.shape[2]
    kern = _drop_alias_arg(_attn_sample_kernel, n_in=6)
    return pl.pallas_call(
        kern,
        grid=(batch,),
        in_specs=[
            pl.BlockSpec((seq, MIX_W), lambda b: (rb0 + b, COL_Q)),
            pl.BlockSpec((seq, MIX_W), lambda b: (rb0 + b, COL_Q + 1)),
            pl.BlockSpec((seq, MIX_W), lambda b: (rb0 + b, COL_Q + 2)),
            pl.BlockSpec((None, None, past, MIX_W), lambda b: (layer, b, 0, 0)),
            pl.BlockSpec((None, None, past, MIX_W), lambda b: (layer, b, 0, 0)),
            pl.BlockSpec((None, HEADS, CHUNK, BAND), lambda b: (layer, 0, 0, 0)),
            pl.BlockSpec(memory_space=pl.ANY),
        ],
        out_specs=[pl.BlockSpec((seq, MIX_W), lambda b: (rb0 + b, 0)),
                   pl.BlockSpec((None, seq, MIX_W), lambda b: (b, 0, 0)),
                   pl.BlockSpec((None, seq, MIX_W), lambda b: (b, 0, 0))],
        out_shape=[jax.ShapeDtypeStruct((proj.shape[0], MIX_W), BF16),
                   jax.ShapeDtypeStruct((batch, seq, MIX_W), F32),
                   jax.ShapeDtypeStruct((batch, seq, MIX_W), F32)],
        scratch_shapes=[pltpu.VMEM((past + seq, MIX_W), BF16), pltpu.VMEM((past + seq, MIX_W), BF16),
                        pltpu.VMEM((seq, MIX_W), F32)],
        input_output_aliases={6: 0},
        compiler_params=_params(("parallel",)),
        name="attn_sample",
    )(proj, proj, proj, cache_k, cache_v, bias, out_prev)


def _pad_rwkv_cols(x):
    o = 3 * MIX_W
    pad = [(0, 0)] * (x.ndim - 1) + [(0, LORA_PAD - W_RANK)]
    return jnp.concatenate([x[..., :o], jnp.pad(x[..., o:o + W_RANK], pad),
                            jnp.pad(x[..., o + W_RANK:o + W_RANK + A_RANK], pad),
                            x[..., o + W_RANK + A_RANK:]], axis=-1)


def _unpad_rwkv_cols(x):
    o = 3 * MIX_W
    return jnp.concatenate([x[..., :o], x[..., o:o + W_RANK], x[..., o + LORA_PAD:o + LORA_PAD + A_RANK],
                            x[..., o + 2 * LORA_PAD:]], axis=-1)


def _pad_rows(x, rows):
    return jnp.pad(x, [(0, 0)] * (x.ndim - 2) + [(0, rows - x.shape[-2]), (0, 0)])


def kernel(x_prompt, x_sample, state_rwkv_shift, state_rwkv_wkv, cache_conv, cache_attn_k, cache_attn_v,
           norm_mix, norm_ffn, norm_final, w_in, rwkv_mu, rwkv_w0, rwkv_w2, rwkv_a0, rwkv_a2, rwkv_g2,
           rwkv_kk, rwkv_ka, rwkv_rk, rwkv_gn_w, rwkv_gn_b, gmlp_ln_w, gmlp_ln_b, gmlp_ws, gmlp_bs,
           conv_dw, conv_dw_b, conv_ln_w, conv_ln_b, attn_rel_bias, w_branch, w_out, w_ffn_in, w_ffn_out):
    bp, seq, d = x_prompt.shape
    bs, dseq, _ = x_sample.shape
    depth = w_in.shape[0]
    rows_p = bp * seq
    rows_s = bs * dseq
    off_gate = RWKV_COLS + 7 * MIX_W
    assert seq % GMLP_CHUNK == 0 and dseq <= CHUNK and rows_p % dseq == 0

    w_mix = jnp.concatenate([_pad_rwkv_cols(w_in[..., :RWKV_COLS]), w_in[..., RWKV_COLS:off_gate]],
                            axis=-1).astype(BF16)
    w_gate = w_in[..., off_gate:].astype(BF16)
    w_branch_b = w_branch.astype(BF16)
    w_out_b = w_out.astype(BF16)
    w_ffn_in_b = w_ffn_in.astype(BF16)
    w_ffn_out_b = w_ffn_out.astype(BF16)
    row3 = lambda p: p.reshape(depth, 1, -1)
    lp = {
        "rwkv_mu": row3(_pad_rwkv_cols(rwkv_mu)), "rwkv_w0": row3(rwkv_w0), "rwkv_a0": row3(rwkv_a0),
        "rwkv_w2": _pad_rows(rwkv_w2, LORA_PAD).astype(BF16), "rwkv_a2": _pad_rows(rwkv_a2, LORA_PAD).astype(BF16),
        "rwkv_g2": rwkv_g2.astype(BF16), "rwkv_kk": row3(rwkv_kk), "rwkv_ka": row3(rwkv_ka),
        "rwkv_rk": row3(rwkv_rk), "rwkv_gn_w": row3(rwkv_gn_w), "rwkv_gn_b": row3(rwkv_gn_b),
        "ones_bd": jnp.kron(jnp.eye(HEADS, dtype=F32), jnp.ones((HEAD, HEAD), F32)).astype(BF16),
        "gmlp_ln_w": row3(gmlp_ln_w), "gmlp_ln_b": row3(gmlp_ln_b), "gmlp_ws": gmlp_ws,
        "gmlp_bias": jnp.repeat(jnp.swapaxes(gmlp_bs, 1, 2), HEAD, axis=2),
        "conv_dw": conv_dw, "conv_dw_b": row3(conv_dw_b), "conv_ln_w": row3(conv_ln_w),
        "conv_ln_b": row3(conv_ln_b),
    }
    g_mix = row3(norm_mix)
    g_ffn = row3(norm_ffn)
    table_pad = jnp.pad(attn_rel_bias, ((0, 0), (0, 0), (0, REL_PAD - attn_rel_bias.shape[-1])))
    bias = jnp.swapaxes(_rel_bias(table_pad), 1, 2)

    past = cache_attn_k.shape[2]
    cache_k = cache_attn_k.reshape(depth, bs, past, MIX_W)
    cache_v = cache_attn_v.reshape(depth, bs, past, MIX_W)
    shift_s = _pad_rwkv_cols(state_rwkv_shift).reshape(depth, bs, 1, RWKV_PAD_COLS)
    conv_s = jnp.pad(cache_conv, ((0, 0), (0, 0), (CONV_TAIL - (CONV_W - 1), 0), (0, 0)))
    shift_p = jnp.zeros((bp, 1, RWKV_PAD_COLS), F32)
    wkv_p = jnp.zeros((bp, HEADS, HEAD, HEAD), F32)
    conv_p = jnp.zeros((bp, CONV_TAIL, MIX_W), F32)

    x = jnp.concatenate([x_prompt.reshape(rows_p, d), x_sample.reshape(rows_s, d)], axis=0)
    keep = min(BAND_PAST, seq)
    conv_tb = _pick_tile(seq, 256, 32)
    rwkv_rows = _pick_tile(seq, RWKV_STEP_ROWS, CHUNK)
    outs = {k: [] for k in ("p_shift", "p_wkv", "p_conv", "p_k", "p_v",
                            "s_shift", "s_wkv", "s_conv", "s_k", "s_v", "s_gv")}
    for l in range(depth):
        proj, h = _inproj(x, g_mix, w_mix, l)

        o_rwkv, wkv_new_p, shift_new_p = _rwkv(proj, None, shift_p, wkv_p, lp, l, row0=0, batch=bp, seq=seq,
                                               chunk=CHUNK, rows=rwkv_rows)
        o_rwkv, wkv_new_s, shift_new_s = _rwkv(proj, o_rwkv, shift_s[l], state_rwkv_wkv[l], lp, l, row0=rows_p,
                                               batch=bs, seq=dseq, chunk=dseq, rows=dseq)
        (o_gmlp,) = _gmlp(proj, None, lp, l, row0=0, batch=bp, seq=seq, chunk=GMLP_CHUNK, emit_v=False)
        o_gmlp, gv_s = _gmlp(proj, o_gmlp, lp, l, row0=rows_p, batch=bs, seq=dseq, chunk=dseq, emit_v=True)
        o_conv, conv_new_p = _conv(proj, None, conv_p, lp, l, row0=0, batch=bp, seq=seq, tb=conv_tb)
        o_conv, conv_new_s = _conv(proj, o_conv, conv_s[l], lp, l, row0=rows_p, batch=bs, seq=dseq, tb=dseq)
        o_attn, k_new_p, v_new_p = _attn_prompt(proj, bias, l, batch=bp, seq=seq, keep=keep)
        o_attn, k_new_s, v_new_s = _attn_sample(proj, o_attn, cache_k, cache_v, bias, l, row0=rows_p, batch=bs,
                                                seq=dseq)

        merged = _merge(h, w_gate, (o_rwkv, o_gmlp, o_conv, o_attn), w_branch_b, l)
        x = _resid(merged, w_out_b, x, l)
        act = _ffn_in(x, g_ffn, w_ffn_in_b, l)
        x = _resid(act, w_ffn_out_b, x, l)

        outs["p_shift"].append(_unpad_rwkv_cols(shift_new_p[:, 0]))
        outs["p_wkv"].append(wkv_new_p)
        outs["p_conv"].append(conv_new_p)
        outs["p_k"].append(k_new_p.reshape(bp, keep, HEADS, HEAD))
        outs["p_v"].append(v_new_p.reshape(bp, keep, HEADS, HEAD))
        outs["s_shift"].append(_unpad_rwkv_cols(shift_new_s[:, 0]))
        outs["s_wkv"].append(wkv_new_s)
        outs["s_conv"].append(conv_new_s)
        outs["s_k"].append(k_new_s.reshape(bs, dseq, HEADS, HEAD))
        outs["s_v"].append(v_new_s.reshape(bs, dseq, HEADS, HEAD))
        outs["s_gv"].append(gv_s.reshape(bs, dseq, MIX_W))

    g_final = norm_final.reshape(1, d)
    y_p = _final_norm(x, g_final, row0=0, rows=rows_p)
    y_s = _final_norm(x, g_final, row0=rows_p, rows=rows_s)
    st = {k: jnp.stack(v) for k, v in outs.items()}
    return (y_p.reshape(bp, seq, d), y_s.reshape(bs, dseq, d),
            st["p_shift"], st["p_wkv"], st["p_conv"], st["p_k"], st["p_v"],
            st["s_shift"], st["s_wkv"], st["s_conv"], st["s_k"], st["s_v"], st["s_gv"])
```

```python
import functools

import jax
import jax.numpy as jnp
from jax import lax
from jax.experimental import pallas as pl
from jax.experimental.pallas import tpu as pltpu

F32 = jnp.float32
BF16 = jnp.bfloat16

MIX_W = 512
N_BRANCH = 4
HEADS = 8
HEAD = 64
W_RANK = 96
A_RANK = 96
G_RANK = 256
RWKV_COLS = 3 * MIX_W + W_RANK + A_RANK + G_RANK
RWKV_PAD_COLS = 4 * MIX_W
GATE_ROW0 = RWKV_COLS + 7 * MIX_W
GN_EPS = 64e-5
RMS_EPS = 1e-6
LN_EPS = 1e-5
CHUNK = 64
GMLP_CHUNK = 128
CONV_W = 31
SUBLANES = 8
CONV_TAIL = 32
BAND_PAST = 8 * CHUNK
BAND = BAND_PAST + CHUNK
REL_CLIP = 128
REL_PAD = 384
MIX_COLS = RWKV_PAD_COLS + 2 * MIX_W + 2 * MIX_W + 3 * MIX_W
COL_GMLP = RWKV_PAD_COLS // (2 * MIX_W)
COL_CONV = COL_GMLP + 1
COL_Q = (RWKV_PAD_COLS + 4 * MIX_W) // MIX_W

VMEM_LIMIT_BYTES = 52 * 1024 * 1024
ROW_TILE_SMALL = 768
ROW_TILE_LARGE = 1408
COL_TILE_TARGET = 512
MIXER_STEP_ROWS = 2 * CHUNK
MXU_TILE = 256


def _pick_tile(n, target, mult):
    best = None
    for t in range(mult, min(n, target) + 1, mult):
        if n % t == 0:
            best = t
    assert best is not None, (n, target, mult)
    return best


def _params(sem):
    return pltpu.CompilerParams(dimension_semantics=sem, vmem_limit_bytes=VMEM_LIMIT_BYTES)


def _rms_bf16(x, g):
    ms = jnp.mean(x * x, axis=-1, keepdims=True)
    return (x * lax.rsqrt(ms + RMS_EPS) * g).astype(BF16)


def _layer_norm(x, w, b):
    mu = jnp.mean(x, axis=-1, keepdims=True)
    var = jnp.mean(jnp.square(x - mu), axis=-1, keepdims=True)
    return (x - mu) * lax.rsqrt(var + LN_EPS) * w + b


def _gelu_tanh(x):
    return 0.5 * x * (1.0 + jnp.tanh(0.7978845608028654 * (x + 0.044715 * x * x * x)))


def _dot(a, b):
    return jnp.dot(a, b, preferred_element_type=F32)


def _dot_nt(a, b):
    return lax.dot_general(a, b, (((1,), (1,)), ((), ())), preferred_element_type=F32)


def _dot_tn(a, b):
    return lax.dot_general(a, b, (((0,), (0,)), ((), ())), preferred_element_type=F32)


def _inproj_kernel(x_ref, g_ref, w_ref, proj_ref, h_ref):
    @pl.when(pl.program_id(1) == 0)
    def _():
        h_ref[...] = _rms_bf16(x_ref[...], g_ref[...])

    proj_ref[...] = _dot_nt(h_ref[...], w_ref[0])


def _inproj(x, g, w_in_t, layer):
    m, d = x.shape
    n = MIX_COLS
    tm = _pick_tile(m, ROW_TILE_LARGE, 16)
    tn = _pick_tile(n, COL_TILE_TARGET, 128)
    assert RWKV_PAD_COLS % tn == 0
    n_rwkv = RWKV_PAD_COLS // tn
    overlap = RWKV_PAD_COLS - RWKV_COLS
    row_start = lambda j: pl.multiple_of(j * tn - jnp.where(j >= n_rwkv, overlap, 0), overlap)
    return pl.pallas_call(
        _inproj_kernel,
        grid=(m // tm, n // tn),
        in_specs=[
            pl.BlockSpec((tm, d), lambda i, j: (i, 0)),
            pl.BlockSpec((None, 1, d), lambda i, j: (layer, 0, 0)),
            pl.BlockSpec((pl.Element(1), pl.Element(tn), pl.Element(d)), lambda i, j: (layer, row_start(j), 0)),
        ],
        out_specs=[
            pl.BlockSpec((tm, tn), lambda i, j: (i, j)),
            pl.BlockSpec((tm, d), lambda i, j: (i, 0)),
        ],
        out_shape=[jax.ShapeDtypeStruct((m, n), F32), jax.ShapeDtypeStruct((m, d), BF16)],
        compiler_params=_params(("parallel", "arbitrary")),
        name="inproj",
    )(x, g, w_in_t)


def _merge_kernel(h_ref, wg0, wg1, wg2, wg3, b0, b1, b2, b3, wb_ref, o_ref):
    h = h_ref[...]
    acc = None
    for n, (wg, br) in enumerate(((wg0, b0), (wg1, b1), (wg2, b2), (wg3, b3))):
        gate = jax.nn.sigmoid(_dot_nt(h, wg[0]))
        term = gate * _dot(br[...], wb_ref[n])
        acc = term if acc is None else acc + term
    o_ref[...] = acc.astype(BF16)


def _merge(h, w_gate, branches, w_branch, layer):
    m, d = h.shape
    tm = _pick_tile(m, ROW_TILE_SMALL, 16)
    tn = _pick_tile(d, COL_TILE_TARGET, 128)
    nj = d // tn
    gate_specs = [pl.BlockSpec((pl.Element(1), pl.Element(tn), pl.Element(d)),
                               functools.partial(lambda i, j, n: (layer, pl.multiple_of(GATE_ROW0 + n * d + j * tn, 64), 0),
                                                 n=n))
                  for n in range(N_BRANCH)]
    br_specs = [pl.BlockSpec((tm, MIX_W), lambda i, j: (i, 0)) for _ in range(N_BRANCH)]
    return pl.pallas_call(
        _merge_kernel,
        grid=(m // tm, nj),
        in_specs=[pl.BlockSpec((tm, d), lambda i, j: (i, 0))] + gate_specs + br_specs
        + [pl.BlockSpec((None, N_BRANCH, MIX_W, tn), lambda i, j: (layer, 0, 0, j))],
        out_specs=pl.BlockSpec((tm, tn), lambda i, j: (i, j)),
        out_shape=jax.ShapeDtypeStruct((m, d), BF16),
        compiler_params=_params(("parallel", "arbitrary")),
        name="merge",
    )(h, w_gate, w_gate, w_gate, w_gate, *branches, w_branch)


def _resid_kernel(a_ref, w_ref, x_ref, o_ref):
    o_ref[...] = x_ref[...] + _dot(a_ref[...], w_ref[...])


def _resid(a, w, x, layer):
    m, k = a.shape
    d = x.shape[-1]
    tm = _pick_tile(m, ROW_TILE_LARGE if 8 * ROW_TILE_LARGE * k * 2 <= VMEM_LIMIT_BYTES else ROW_TILE_SMALL, 16)
    tn = _pick_tile(d, COL_TILE_TARGET, 128)
    return pl.pallas_call(
        _resid_kernel,
        grid=(m // tm, d // tn),
        in_specs=[
            pl.BlockSpec((tm, k), lambda i, j: (i, 0)),
            pl.BlockSpec((None, k, tn), lambda i, j: (layer, 0, j)),
            pl.BlockSpec((tm, tn), lambda i, j: (i, j)),
        ],
        out_specs=pl.BlockSpec((tm, tn), lambda i, j: (i, j)),
        out_shape=jax.ShapeDtypeStruct((m, d), F32),
        compiler_params=_params(("parallel", "arbitrary")),
        name="resid",
    )(a, w, x)


def _ffn_in_kernel(x_ref, g_ref, wg_ref, wu_ref, act_ref, h_scr):
    @pl.when(pl.program_id(1) == 0)
    def _():
        h_scr[...] = _rms_bf16(x_ref[...], g_ref[...])

    h = h_scr[...]
    act_ref[...] = (jax.nn.silu(_dot(h, wg_ref[...])) * _dot(h, wu_ref[...])).astype(BF16)


def _ffn_in(x, g, w_ffn_in, layer):
    m, d = x.shape
    dff = w_ffn_in.shape[-1] // 2
    tm = _pick_tile(m, ROW_TILE_LARGE, 16)
    tn = _pick_tile(dff, COL_TILE_TARGET, 128)
    nj = dff // tn
    return pl.pallas_call(
        _ffn_in_kernel,
        grid=(m // tm, nj),
        in_specs=[
            pl.BlockSpec((tm, d), lambda i, j: (i, 0)),
            pl.BlockSpec((None, 1, d), lambda i, j: (layer, 0, 0)),
            pl.BlockSpec((None, d, tn), lambda i, j: (layer, 0, j)),
            pl.BlockSpec((None, d, tn), lambda i, j: (layer, 0, nj + j)),
        ],
        out_specs=pl.BlockSpec((tm, tn), lambda i, j: (i, j)),
        out_shape=jax.ShapeDtypeStruct((m, dff), BF16),
        scratch_shapes=[pltpu.VMEM((tm, d), BF16)],
        compiler_params=_params(("parallel", "arbitrary")),
        name="ffn_in",
    )(x, g, w_ffn_in, w_ffn_in)


def _final_norm_kernel(x_ref, g_ref, o_ref):
    x = x_ref[...]
    ms = jnp.mean(x * x, axis=-1, keepdims=True)
    o_ref[...] = x * lax.rsqrt(ms + RMS_EPS) * g_ref[...]


def _final_norm(x, g, *, row0, rows):
    d = x.shape[-1]
    tm = _pick_tile(rows, ROW_TILE_SMALL, 16)
    assert row0 % tm == 0
    rb0 = row0 // tm
    return pl.pallas_call(
        _final_norm_kernel,
        grid=(rows // tm,),
        in_specs=[pl.BlockSpec((tm, d), lambda i: (rb0 + i, 0)), pl.BlockSpec((1, d), lambda i: (0, 0))],
        out_specs=pl.BlockSpec((tm, d), lambda i: (i, 0)),
        out_shape=jax.ShapeDtypeStruct((rows, d), F32),
        compiler_params=_params(("parallel",)),
        name="final_norm",
    )(x, g)


def _split3(x):
    hi = x.astype(BF16)
    r1 = x - hi.astype(F32)
    mid = r1.astype(BF16)
    lo = (r1 - mid.astype(F32)).astype(BF16)
    return hi, mid, lo


def _head_sum(x, ones_bd):
    rows = x.shape[0]
    w = ones_bd.shape[0]
    hi = x.astype(BF16)
    lo = (x - hi.astype(F32)).astype(BF16)
    st = jnp.concatenate([hi, lo], axis=0)
    s = jnp.concatenate([_dot(st[:, j:j + w], ones_bd) for j in range(0, x.shape[1], w)], axis=1)
    return s[:rows] + s[rows:]


def _with_tail(main, o_ref, tail_ref, n_main):
    i = pl.program_id(0)
    if tail_ref is None:
        main(i)
        return

    @pl.when(i < n_main)
    def _():
        main(i)

    @pl.when(i == n_main)
    def _():
        o_ref[...] = tail_ref[...]


def _rwkv_kernel(p_ref, shift0_ref, s0_ref, mu_ref, w0_ref, w2_ref, a0_ref, a2_ref, g2_ref, kkw_ref, ka_ref,
                 rk_ref, gnw_ref, gnb_ref, ones_ref, tri_ref, *rest, nc, L, n_main, has_tail):
    tail_ref = rest[0] if has_tail else None
    o_ref, sout_ref, shift_out_ref, carry_ref, s_ref, y_ref = rest[1:] if has_tail else rest
    main = functools.partial(_rwkv_step, p_ref, shift0_ref, s0_ref, mu_ref, w0_ref, w2_ref, a0_ref, a2_ref, g2_ref,
                             kkw_ref, ka_ref, rk_ref, gnw_ref, gnb_ref, ones_ref, tri_ref, o_ref, sout_ref,
                             shift_out_ref, carry_ref, s_ref, y_ref, nc=nc, L=L)
    _with_tail(main, o_ref, tail_ref, n_main)


def _rwkv_step(p_ref, shift0_ref, s0_ref, mu_ref, w0_ref, w2_ref, a0_ref, a2_ref, g2_ref, kkw_ref, ka_ref,
               rk_ref, gnw_ref, gnb_ref, ones_ref, tri_ref, o_ref, sout_ref, shift_out_ref,
               carry_ref, s_ref, y_ref, step, *, nc, L):
    rows = p_ref.shape[0]
    ncb = rows // L
    c = lax.rem(step, nc)

    @pl.when(c == 0)
    def _():
        carry_ref[...] = shift0_ref[...]
        s_ref[...] = s0_ref[...]

    p = p_ref[...]
    row = lax.broadcasted_iota(jnp.int32, p.shape, 0)
    prev = jnp.where(row == 0, carry_ref[...], pltpu.roll(p, 1, 0))
    carry_ref[...] = p[rows - 1:rows, :]
    xs = p + mu_ref[...] * (prev - p)

    o_w = 3 * MIX_W
    r = xs[:, 0:MIX_W]
    k = xs[:, MIX_W:2 * MIX_W]
    v = xs[:, 2 * MIX_W:3 * MIX_W]
    wl = xs[:, o_w:o_w + w2_ref.shape[0]]
    al = xs[:, o_w:o_w + a2_ref.shape[0]]
    gl = xs[:, o_w + MIX_W - g2_ref.shape[0]:o_w + MIX_W]
    ones_bd = ones_ref[...]

    zz = w0_ref[...] + _dot(jnp.tanh(wl).astype(BF16), w2_ref[...])
    nz = -zz
    softplus = jnp.maximum(nz, 0.0) + jnp.log(1.0 + jnp.exp(-jnp.abs(nz)))
    lw = -jnp.exp(-softplus - 0.5)
    a = jax.nn.sigmoid(a0_ref[...] + _dot(al.astype(BF16), a2_ref[...]))
    g = _dot(jax.nn.sigmoid(gl).astype(BF16), g2_ref[...])
    kk = k * kkw_ref[...]
    kk = kk * lax.rsqrt(jnp.maximum(_head_sum(kk * kk, ones_bd), 1e-24))
    k = k * (1.0 + (a - 1.0) * ka_ref[...])
    a_s = -kk
    b_s = kk * a
    bonus = _head_sum(r * k * rk_ref[...], ones_bd) * v

    cum3 = _dot(tri_ref[...], jnp.concatenate(_split3(lw), axis=1))
    cum = cum3[:, :MIX_W] + cum3[:, MIX_W:2 * MIX_W] + cum3[:, 2 * MIX_W:]
    ends = [cum[(ch + 1) * L - 1:(ch + 1) * L, :] for ch in range(ncb)]
    cl = jnp.concatenate([jnp.broadcast_to(e, (L, MIX_W)) for e in ends], axis=0) if ncb > 1 else ends[0]
    inv_p = jnp.exp(-cum)
    to_end = jnp.exp(cl - cum)
    at_f = a_s * jnp.exp(cum - lw)
    at = at_f.astype(BF16)
    bt = (b_s * inv_p).astype(BF16)
    kt = (k * inv_p).astype(BF16)
    rt = (r * jnp.exp(cum)).astype(BF16)
    bh = (b_s * to_end).astype(BF16)
    kh = (k * to_end).astype(BF16)
    p_end = [jnp.exp(e) for e in ends]
    vb = v.astype(BF16)

    ri = lax.broadcasted_iota(jnp.int32, (L, 2 * L), 0)
    ci = lax.broadcasted_iota(jnp.int32, (L, 2 * L), 1)
    ci = jnp.where(ci >= L, ci - L, ci)
    strict = ci < ri
    incl = ci <= ri

    chains = [(ch, h) for ch in range(ncb) for h in range(HEADS)]
    sl = lambda arr, ch, h: arr[ch * L:(ch + 1) * L, h * HEAD:(h + 1) * HEAD]
    gm = [_dot_nt(jnp.concatenate([sl(at, ch, h), sl(rt, ch, h)], axis=0),
                  jnp.concatenate([sl(kt, ch, h), sl(bt, ch, h)], axis=0)) for ch, h in chains]
    a_top = [jnp.where(strict, m[:L], 0.0) for m in gm]
    m_low = [jnp.where(incl, m[L:], 0.0).astype(BF16) for m in gm]
    q = [t[:, L:] for t in a_top]
    x = [jnp.concatenate([t[:, :L], sl(at_f, ch, h)], axis=1) for t, (ch, h) in zip(a_top, chains)]
    n_stage = L.bit_length() - 1
    for i in range(n_stage):
        last = i == n_stage - 1
        qb = [t.astype(BF16) for t in q]
        rhs = [t.astype(BF16) for t in x] if last else \
              [jnp.concatenate([t.astype(BF16), u], axis=1) for t, u in zip(x, qb)]
        res = [_dot(u, t) for u, t in zip(qb, rhs)]
        x = [t + u[:, :L + HEAD] for t, u in zip(x, res)]
        if not last:
            q = [u[:, L + HEAD:] for u in res]
    u_free = [_dot(t[:, :L].astype(BF16), sl(vb, ch, h)) for t, (ch, h) in zip(x, chains)]
    wm_rt = [jnp.concatenate([t[:, L:].astype(BF16), sl(rt, ch, h)], axis=0) for t, (ch, h) in zip(x, chains)]
    kb = [jnp.concatenate([sl(kh, ch, h), sl(bh, ch, h)], axis=0) for ch, h in chains]

    state = [s_ref[h] for h in range(HEADS)]
    for ch in range(ncb):
        ids = [ch * HEADS + h for h in range(HEADS)]
        ws = [_dot_nt(wm_rt[i], state[h].astype(BF16)) for h, i in enumerate(ids)]
        uv = [jnp.concatenate([sl(vb, ch, h), (ws[h][:L] + u_free[i]).astype(BF16)], axis=0)
              for h, i in enumerate(ids)]
        for h, i in enumerate(ids):
            y_ref[ch * L:(ch + 1) * L, h * HEAD:(h + 1) * HEAD] = ws[h][L:] + _dot(m_low[i], uv[h])
        state = [state[h] * p_end[ch][:, h * HEAD:(h + 1) * HEAD] + _dot_tn(uv[h], kb[i])
                 for h, i in enumerate(ids)]
    for h in range(HEADS):
        s_ref[h] = state[h]

    y = y_ref[...]
    mu_y = _head_sum(y, ones_bd) * (1.0 / HEAD)
    yc = y - mu_y
    var_y = _head_sum(yc * yc, ones_bd) * (1.0 / HEAD)
    yn = yc * lax.rsqrt(var_y + GN_EPS) * gnw_ref[...] + gnb_ref[...]
    o_ref[...] = ((yn + bonus) * g).astype(BF16)

    @pl.when(c == nc - 1)
    def _():
        sout_ref[...] = s_ref[...]
        shift_out_ref[...] = p[rows - 1:rows, :]


def _tail_plan(n_main, rows, tail):
    if tail is None:
        return n_main, (lambda i: i), [], []
    assert tail.shape == (rows, MIX_W), (tail.shape, rows)
    return (n_main + 1, (lambda i: jnp.minimum(i, n_main - 1)),
            [pl.BlockSpec((rows, MIX_W), lambda i: (0, 0))], [tail])


def _rwkv(proj, tail, shift0, s0, lp, layer, *, row0, batch, seq, chunk, rows):
    nc = seq // rows
    rb0 = row0 // rows
    n_main = batch * nc
    n_steps, clamp, tail_specs, tail_args = _tail_plan(n_main, rows, tail)
    seq_of = lambda i: clamp(i) // nc
    vec = lambda name: pl.BlockSpec((None, 1, lp[name].shape[-1]), lambda i: (layer, 0, 0))
    mat = lambda name: pl.BlockSpec((None,) + lp[name].shape[1:], lambda i: (layer, 0, 0))
    const = lambda arr: pl.BlockSpec(arr.shape, lambda i: (0,) * arr.ndim)
    tri = jnp.kron(jnp.eye(rows // chunk, dtype=F32), jnp.tril(jnp.ones((chunk, chunk), F32))).astype(BF16)
    ones_bd = lp["ones_bd"]
    in_specs = [
        pl.BlockSpec((rows, RWKV_PAD_COLS), lambda i: (rb0 + clamp(i), 0)),
        pl.BlockSpec((None, 1, RWKV_PAD_COLS), lambda i: (seq_of(i), 0, 0)),
        pl.BlockSpec((None, HEADS, HEAD, HEAD), lambda i: (seq_of(i), 0, 0, 0)),
        vec("rwkv_mu"), vec("rwkv_w0"), mat("rwkv_w2"), vec("rwkv_a0"), mat("rwkv_a2"), mat("rwkv_g2"),
        vec("rwkv_kk"), vec("rwkv_ka"), vec("rwkv_rk"), vec("rwkv_gn_w"), vec("rwkv_gn_b"),
        const(ones_bd), const(tri),
    ] + tail_specs
    args = [proj, shift0, s0, lp["rwkv_mu"], lp["rwkv_w0"], lp["rwkv_w2"], lp["rwkv_a0"], lp["rwkv_a2"],
            lp["rwkv_g2"], lp["rwkv_kk"], lp["rwkv_ka"], lp["rwkv_rk"], lp["rwkv_gn_w"], lp["rwkv_gn_b"],
            ones_bd, tri] + tail_args
    return pl.pallas_call(
        functools.partial(_rwkv_kernel, nc=nc, L=chunk, n_main=n_main, has_tail=tail is not None),
        grid=(n_steps,),
        in_specs=in_specs,
        out_specs=[pl.BlockSpec((rows, MIX_W), lambda i: (i, 0)),
                   pl.BlockSpec((None, HEADS, HEAD, HEAD), lambda i: (seq_of(i), 0, 0, 0)),
                   pl.BlockSpec((None, 1, RWKV_PAD_COLS), lambda i: (seq_of(i), 0, 0))],
        out_shape=[jax.ShapeDtypeStruct((n_steps * rows, MIX_W), BF16),
                   jax.ShapeDtypeStruct((batch, HEADS, HEAD, HEAD), F32),
                   jax.ShapeDtypeStruct((batch, 1, RWKV_PAD_COLS), F32)],
        scratch_shapes=[pltpu.VMEM((1, RWKV_PAD_COLS), F32), pltpu.VMEM((HEADS, HEAD, HEAD), F32),
                        pltpu.VMEM((rows, MIX_W), F32)],
        compiler_params=_params(("arbitrary",)),
        name="rwkv",
    )(*args)


def _gmlp_kernel(p_ref, lnw_ref, lnb_ref, ws_ref, bias_ref, *rest, emit_v, n_main, has_tail):
    tail_ref = rest[0] if has_tail else None
    rest = rest[1:] if has_tail else rest
    main = functools.partial(_gmlp_step, p_ref, lnw_ref, lnb_ref, ws_ref, bias_ref, *rest, emit_v=emit_v)
    _with_tail(main, rest[0], tail_ref, n_main)


def _gmlp_step(p_ref, lnw_ref, lnb_ref, ws_ref, bias_ref, o_ref, *rest, emit_v):
    if emit_v:
        v_ref, m_ref, _ = rest
    else:
        m_ref, _ = rest
    L = p_ref.shape[0]
    p = p_ref[...]
    u = _gelu_tanh(p[:, :MIX_W])
    v = _layer_norm(_gelu_tanh(p[:, MIX_W:]), lnw_ref[...], lnb_ref[...])
    if emit_v:
        v_ref[...] = v
    vb = v.astype(BF16)
    lower = lax.broadcasted_iota(jnp.int32, (L, L), 1) <= lax.broadcasted_iota(jnp.int32, (L, L), 0)
    for g in range(HEADS):
        gs = slice(g * HEAD, (g + 1) * HEAD)
        w = jnp.where(lower, ws_ref[g, :L, :L], 0.0).astype(BF16)
        m_ref[:, gs] = _dot(w, vb[:, gs])
    o_ref[...] = (u * (m_ref[...] + bias_ref[...])).astype(BF16)


def _gmlp(proj, tail, lp, layer, *, row0, batch, seq, chunk, emit_v):
    rb0 = row0 // chunk
    n_main = batch * seq // chunk
    n_steps, clamp, tail_specs, tail_args = _tail_plan(n_main, chunk, tail)
    bias = lp["gmlp_bias"][:, :chunk, :]
    in_specs = [
        pl.BlockSpec((chunk, 2 * MIX_W), lambda i: (rb0 + clamp(i), COL_GMLP)),
        pl.BlockSpec((None, 1, MIX_W), lambda i: (layer, 0, 0)),
        pl.BlockSpec((None, 1, MIX_W), lambda i: (layer, 0, 0)),
        pl.BlockSpec((None, HEADS, GMLP_CHUNK, GMLP_CHUNK), lambda i: (layer, 0, 0, 0)),
        pl.BlockSpec((None, chunk, MIX_W), lambda i: (layer, 0, 0)),
    ] + tail_specs
    args = [proj, lp["gmlp_ln_w"], lp["gmlp_ln_b"], lp["gmlp_ws"], bias] + tail_args
    out_specs = [pl.BlockSpec((chunk, MIX_W), lambda i: (i, 0))]
    out_shape = [jax.ShapeDtypeStruct((n_steps * chunk, MIX_W), BF16)]
    if emit_v:
        out_specs.append(pl.BlockSpec((chunk, MIX_W), lambda i: (clamp(i), 0)))
        out_shape.append(jax.ShapeDtypeStruct((n_main * chunk, MIX_W), F32))
    return pl.pallas_call(
        functools.partial(_gmlp_kernel, emit_v=emit_v, n_main=n_main, has_tail=tail is not None),
        grid=(n_steps,),
        in_specs=in_specs,
        out_specs=out_specs,
        out_shape=out_shape,
        scratch_shapes=[pltpu.VMEM((chunk, MIX_W), F32)],
        compiler_params=_params(("arbitrary",)),
        name="gmlp",
    )(*args)


def _conv_kernel(p_ref, prev_ref, dw_ref, dwb_ref, lnw_ref, lnb_ref, *rest, nt, n_main, has_tail):
    tail_ref = rest[0] if has_tail else None
    rest = rest[1:] if has_tail else rest
    main = functools.partial(_conv_step, p_ref, prev_ref, dw_ref, dwb_ref, lnw_ref, lnb_ref, *rest, nt=nt)
    _with_tail(main, rest[0], tail_ref, n_main)


def _conv_step(p_ref, prev_ref, dw_ref, dwb_ref, lnw_ref, lnb_ref, o_ref, state_ref, z_ref, step, *, nt):
    tb = p_ref.shape[0]
    t = lax.rem(step, nt)
    off = CONV_TAIL - (CONV_W - 1)

    @pl.when(t == 0)
    def _():
        z_ref[0, 0:CONV_TAIL, :] = prev_ref[...]

    @pl.when(t > 0)
    def _():
        z_ref[0, 0:CONV_TAIL, :] = z_ref[0, tb:tb + CONV_TAIL, :]

    p = p_ref[...]
    z_ref[0, CONV_TAIL:CONV_TAIL + tb, :] = p[:, :MIX_W] * jax.nn.sigmoid(p[:, MIX_W:])
    span = tb + CONV_TAIL - SUBLANES
    for s in range(1, SUBLANES):
        z_ref[s, 0:span, :] = z_ref[0, s:s + span, :]
    sub = min(tb, 32)
    for r0 in range(0, tb, sub):
        acc = None
        for w in range(CONV_W):
            a, s = divmod(w + off, SUBLANES)
            term = z_ref[s, r0 + a * SUBLANES:r0 + a * SUBLANES + sub, :] * dw_ref[w:w + 1, :]
            acc = term if acc is None else acc + term
        y = _layer_norm(acc + dwb_ref[...], lnw_ref[...], lnb_ref[...])
        o_ref[r0:r0 + sub, :] = jax.nn.silu(y).astype(BF16)

    @pl.when(t == nt - 1)
    def _():
        state_ref[...] = z_ref[0, tb + off:tb + CONV_TAIL, :]


def _conv(proj, tail, prev, lp, layer, *, row0, batch, seq, tb):
    nt = seq // tb
    rb0 = row0 // tb
    n_main = batch * nt
    n_steps, clamp, tail_specs, tail_args = _tail_plan(n_main, tb, tail)
    seq_of = lambda i: clamp(i) // nt
    in_specs = [
        pl.BlockSpec((tb, 2 * MIX_W), lambda i: (rb0 + clamp(i), COL_CONV)),
        pl.BlockSpec((None, CONV_TAIL, MIX_W), lambda i: (seq_of(i), 0, 0)),
        pl.BlockSpec((None, CONV_W, MIX_W), lambda i: (layer, 0, 0)),
        pl.BlockSpec((None, 1, MIX_W), lambda i: (layer, 0, 0)),
        pl.BlockSpec((None, 1, MIX_W), lambda i: (layer, 0, 0)),
        pl.BlockSpec((None, 1, MIX_W), lambda i: (layer, 0, 0)),
    ] + tail_specs
    args = [proj, prev, lp["conv_dw"], lp["conv_dw_b"], lp["conv_ln_w"], lp["conv_ln_b"]] + tail_args
    return pl.pallas_call(
        functools.partial(_conv_kernel, nt=nt, n_main=n_main, has_tail=tail is not None),
        grid=(n_steps,),
        in_specs=in_specs,
        out_specs=[pl.BlockSpec((tb, MIX_W), lambda i: (i, 0)),
                   pl.BlockSpec((None, CONV_W - 1, MIX_W), lambda i: (seq_of(i), 0, 0))],
        out_shape=[jax.ShapeDtypeStruct((n_steps * tb, MIX_W), BF16),
                   jax.ShapeDtypeStruct((batch, CONV_W - 1, MIX_W), F32)],
        scratch_shapes=[pltpu.VMEM((SUBLANES, tb + CONV_TAIL, MIX_W), F32)],
        compiler_params=_params(("arbitrary",)),
        name="conv",
    )(*args)


def _rel_bias_kernel(table_ref, o_ref):
    width = BAND + CHUNK
    u = lax.broadcasted_iota(jnp.int32, (REL_PAD, width), 1)
    m = lax.broadcasted_iota(jnp.int32, (REL_PAD, width), 0)
    idx = jnp.clip(BAND - 1 - u, -REL_CLIP, REL_CLIP) + REL_CLIP
    onehot = (idx == m).astype(F32)
    ext = jnp.dot(table_ref[...], onehot, preferred_element_type=F32, precision=lax.Precision.HIGHEST)
    for i in range(CHUNK):
        o_ref[i] = ext[:, CHUNK - 1 - i:CHUNK - 1 - i + BAND]


def _rel_bias(table_pad):
    depth = table_pad.shape[0]
    return pl.pallas_call(
        _rel_bias_kernel,
        grid=(depth,),
        in_specs=[pl.BlockSpec((None, HEADS, REL_PAD), lambda l: (l, 0, 0))],
        out_specs=pl.BlockSpec((None, CHUNK, HEADS, BAND), lambda l: (l, 0, 0, 0)),
        out_shape=jax.ShapeDtypeStruct((depth, CHUNK, HEADS, BAND), F32),
        compiler_params=_params(("parallel",)),
        name="rel_bias",
    )(table_pad)


def _attend_heads(q, k, v, bias_of, valid):
    rq = q.shape[0]
    pair_w = 2 * HEAD
    first = lax.broadcasted_iota(jnp.int32, (rq, pair_w), 1) < HEAD
    zero = jnp.zeros((rq, pair_w), q.dtype)
    units = []
    for pr in range(HEADS // 2):
        ps = slice(pr * pair_w, (pr + 1) * pair_w)
        for half in range(2):
            units.append((2 * pr + half, jnp.where(first if half == 0 else ~first, q[:, ps], zero), k[:, ps], v[:, ps]))
    s = [_dot_nt(qm, k2) * (HEAD ** -0.5) + bias_of(h) for h, qm, k2, _ in units]
    if valid is not None:
        s = [jnp.where(valid, t, -1e30) for t in s]
    e = [jnp.exp(t - jnp.max(t, axis=-1, keepdims=True)) for t in s]
    pr_ = [(t * (1.0 / jnp.sum(t, axis=-1, keepdims=True))).astype(BF16) for t in e]
    o = [_dot(t, v2) for t, (_, _, _, v2) in zip(pr_, units)]
    return jnp.concatenate([jnp.where(first, o[2 * pr], o[2 * pr + 1]) for pr in range(HEADS // 2)], axis=1)


def _attn_prompt_kernel(q_ref, k_ref, v_ref, bias_ref, *rest, nq, n_main, has_tail):
    tail_ref = rest[0] if has_tail else None
    rest = rest[1:] if has_tail else rest
    main = functools.partial(_attn_prompt_step, q_ref, k_ref, v_ref, bias_ref, *rest, nq=nq)
    _with_tail(main, rest[0], tail_ref, n_main)


def _attn_prompt_step(q_ref, k_ref, v_ref, bias_ref, o_ref, knew_ref, vnew_ref, kp_ref, vp_ref, step, *, nq):
    rows = q_ref.shape[0]
    seq = k_ref.shape[0]
    keep = knew_ref.shape[0]
    c = lax.rem(step, nq)

    @pl.when(c == 0)
    def _():
        kp_ref[0:BAND_PAST, :] = jnp.zeros((BAND_PAST, MIX_W), BF16)
        vp_ref[0:BAND_PAST, :] = jnp.zeros((BAND_PAST, MIX_W), BF16)
        kp_ref[BAND_PAST:, :] = k_ref[...].astype(BF16)
        vp_ref[BAND_PAST:, :] = v_ref[...].astype(BF16)
        knew_ref[...] = k_ref[seq - keep:, :]
        vnew_ref[...] = v_ref[seq - keep:, :]

    col = lax.broadcasted_iota(jnp.int32, (CHUNK, BAND), 1)
    outs = []
    for cb in range(rows // CHUNK):
        r0 = pl.multiple_of(c * rows + cb * CHUNK, CHUNK)
        q = q_ref[cb * CHUNK:(cb + 1) * CHUNK, :].astype(BF16)
        outs.append(_attend_heads(q, kp_ref[pl.ds(r0, BAND), :], vp_ref[pl.ds(r0, BAND), :],
                                  lambda h: bias_ref[h], (r0 + col) >= BAND_PAST))
    o_ref[...] = jnp.concatenate(outs, axis=0).astype(BF16)


def _attn_prompt(proj, tail, bias, layer, *, batch, seq, keep, rows):
    nq = seq // rows
    n_main = batch * nq
    n_steps, clamp, tail_specs, tail_args = _tail_plan(n_main, rows, tail)
    seq_of = lambda i: clamp(i) // nq
    return pl.pallas_call(
        functools.partial(_attn_prompt_kernel, nq=nq, n_main=n_main, has_tail=tail is not None),
        grid=(n_steps,),
        in_specs=[
            pl.BlockSpec((rows, MIX_W), lambda i: (clamp(i), COL_Q)),
            pl.BlockSpec((seq, MIX_W), lambda i: (seq_of(i), COL_Q + 1)),
            pl.BlockSpec((seq, MIX_W), lambda i: (seq_of(i), COL_Q + 2)),
            pl.BlockSpec((None, HEADS, CHUNK, BAND), lambda i: (layer, 0, 0, 0)),
        ] + tail_specs,
        out_specs=[pl.BlockSpec((rows, MIX_W), lambda i: (i, 0)),
                   pl.BlockSpec((None, keep, MIX_W), lambda i: (seq_of(i), 0, 0)),
                   pl.BlockSpec((None, keep, MIX_W), lambda i: (seq_of(i), 0, 0))],
        out_shape=[jax.ShapeDtypeStruct((n_steps * rows, MIX_W), BF16),
                   jax.ShapeDtypeStruct((batch, keep, MIX_W), F32),
                   jax.ShapeDtypeStruct((batch, keep, MIX_W), F32)],
        scratch_shapes=[pltpu.VMEM((BAND_PAST + seq, MIX_W), BF16), pltpu.VMEM((BAND_PAST + seq, MIX_W), BF16)],
        compiler_params=_params(("arbitrary",)),
        name="attn_prompt",
    )(proj, proj, proj, bias, *tail_args)


def _attn_sample_kernel(q_ref, k_ref, v_ref, ck_ref, cv_ref, bias_ref, o_ref, knew_ref, vnew_ref, kp_ref, vp_ref):
    t = q_ref.shape[0]
    past = ck_ref.shape[0]
    kp_ref[0:past, :] = ck_ref[...].astype(BF16)
    vp_ref[0:past, :] = cv_ref[...].astype(BF16)
    kp_ref[past:, :] = k_ref[...].astype(BF16)
    vp_ref[past:, :] = v_ref[...].astype(BF16)
    knew_ref[...] = k_ref[...]
    vnew_ref[...] = v_ref[...]
    o = _attend_heads(q_ref[...].astype(BF16), kp_ref[...], vp_ref[...], lambda h: bias_ref[h, :t, :past + t], None)
    o_ref[...] = o.astype(BF16)


def _attn_sample(proj, cache_k, cache_v, bias, layer, *, row0, batch, seq):
    rb0 = row0 // seq
    past = cache_k.shape[2]
    return pl.pallas_call(
        _attn_sample_kernel,
        grid=(batch,),
        in_specs=[
            pl.BlockSpec((seq, MIX_W), lambda b: (rb0 + b, COL_Q)),
            pl.BlockSpec((seq, MIX_W), lambda b: (rb0 + b, COL_Q + 1)),
            pl.BlockSpec((seq, MIX_W), lambda b: (rb0 + b, COL_Q + 2)),
            pl.BlockSpec((None, None, past, MIX_W), lambda b: (layer, b, 0, 0)),
            pl.BlockSpec((None, None, past, MIX_W), lambda b: (layer, b, 0, 0)),
            pl.BlockSpec((None, HEADS, CHUNK, BAND), lambda b: (layer, 0, 0, 0)),
        ],
        out_specs=[pl.BlockSpec((seq, MIX_W), lambda b: (b, 0)),
                   pl.BlockSpec((None, seq, MIX_W), lambda b: (b, 0, 0)),
                   pl.BlockSpec((None, seq, MIX_W), lambda b: (b, 0, 0))],
        out_shape=[jax.ShapeDtypeStruct((batch * seq, MIX_W), BF16),
                   jax.ShapeDtypeStruct((batch, seq, MIX_W), F32),
                   jax.ShapeDtypeStruct((batch, seq, MIX_W), F32)],
        scratch_shapes=[pltpu.VMEM((past + seq, MIX_W), BF16), pltpu.VMEM((past + seq, MIX_W), BF16)],
        compiler_params=_params(("parallel",)),
        name="attn_sample",
    )(proj, proj, proj, cache_k, cache_v, bias)


def _pad_rwkv_cols(x):
    return jnp.pad(x, [(0, 0)] * (x.ndim - 1) + [(0, RWKV_PAD_COLS - RWKV_COLS)])


def _lora_rows(w, start, first, last):
    return jnp.pad(w, ((0, 0), (start - first, last - start - w.shape[1]), (0, 0))).astype(BF16)


def kernel(x_prompt, x_sample, state_rwkv_shift, state_rwkv_wkv, cache_conv, cache_attn_k, cache_attn_v,
           norm_mix, norm_ffn, norm_final, w_in, rwkv_mu, rwkv_w0, rwkv_w2, rwkv_a0, rwkv_a2, rwkv_g2,
           rwkv_kk, rwkv_ka, rwkv_rk, rwkv_gn_w, rwkv_gn_b, gmlp_ln_w, gmlp_ln_b, gmlp_ws, gmlp_bs,
           conv_dw, conv_dw_b, conv_ln_w, conv_ln_b, attn_rel_bias, w_branch, w_out, w_ffn_in, w_ffn_out):
    bp, seq, d = x_prompt.shape
    bs, dseq, _ = x_sample.shape
    depth = w_in.shape[0]
    rows_p = bp * seq
    rows_s = bs * dseq
    assert seq % GMLP_CHUNK == 0 and dseq <= CHUNK and rows_p % dseq == 0

    w_mix = w_gate = jnp.swapaxes(w_in, 1, 2).astype(BF16)
    w_branch_b = w_branch.astype(BF16)
    w_out_b = w_out.astype(BF16)
    w_ffn_in_b = w_ffn_in.astype(BF16)
    w_ffn_out_b = w_ffn_out.astype(BF16)
    row3 = lambda p: p.reshape(depth, 1, -1)
    lane_tile = 128
    lp = {
        "rwkv_mu": row3(_pad_rwkv_cols(rwkv_mu)), "rwkv_w0": row3(rwkv_w0), "rwkv_a0": row3(rwkv_a0),
        "rwkv_w2": _lora_rows(rwkv_w2, 0, 0, lane_tile),
        "rwkv_a2": _lora_rows(rwkv_a2, W_RANK, 0, 2 * lane_tile),
        "rwkv_g2": _lora_rows(rwkv_g2, W_RANK + A_RANK, lane_tile, MIX_W),
        "rwkv_kk": row3(rwkv_kk), "rwkv_ka": row3(rwkv_ka),
        "rwkv_rk": row3(rwkv_rk), "rwkv_gn_w": row3(rwkv_gn_w), "rwkv_gn_b": row3(rwkv_gn_b),
        "ones_bd": jnp.kron(jnp.eye(MXU_TILE // HEAD, dtype=F32), jnp.ones((HEAD, HEAD), F32)).astype(BF16),
        "gmlp_ln_w": row3(gmlp_ln_w), "gmlp_ln_b": row3(gmlp_ln_b), "gmlp_ws": gmlp_ws,
        "gmlp_bias": jnp.repeat(jnp.swapaxes(gmlp_bs, 1, 2), HEAD, axis=2),
        "conv_dw": conv_dw, "conv_dw_b": row3(conv_dw_b), "conv_ln_w": row3(conv_ln_w),
        "conv_ln_b": row3(conv_ln_b),
    }
    g_mix = row3(norm_mix)
    g_ffn = row3(norm_ffn)
    table_pad = jnp.pad(attn_rel_bias, ((0, 0), (0, 0), (0, REL_PAD - attn_rel_bias.shape[-1])))
    bias = jnp.swapaxes(_rel_bias(table_pad), 1, 2)

    past = cache_attn_k.shape[2]
    cache_k = cache_attn_k.reshape(depth, bs, past, MIX_W)
    cache_v = cache_attn_v.reshape(depth, bs, past, MIX_W)
    shift_s = _pad_rwkv_cols(state_rwkv_shift).reshape(depth, bs, 1, RWKV_PAD_COLS)
    conv_s = jnp.pad(cache_conv, ((0, 0), (0, 0), (CONV_TAIL - (CONV_W - 1), 0), (0, 0)))
    shift_p = jnp.zeros((bp, 1, RWKV_PAD_COLS), F32)
    wkv_p = jnp.zeros((bp, HEADS, HEAD, HEAD), F32)
    conv_p = jnp.zeros((bp, CONV_TAIL, MIX_W), F32)

    x = jnp.concatenate([x_prompt.reshape(rows_p, d), x_sample.reshape(rows_s, d)], axis=0)
    keep = min(BAND_PAST, seq)
    assert rows_s == MIXER_STEP_ROWS and seq % MIXER_STEP_ROWS == 0
    outs = {k: [] for k in ("p_shift", "p_wkv", "p_conv", "p_k", "p_v",
                            "s_shift", "s_wkv", "s_conv", "s_k", "s_v", "s_gv")}
    for l in range(depth):
        proj, h = _inproj(x, g_mix, w_mix, l)

        s_rwkv, wkv_new_s, shift_new_s = _rwkv(proj, None, shift_s[l], state_rwkv_wkv[l], lp, l, row0=rows_p,
                                               batch=bs, seq=dseq, chunk=dseq, rows=dseq)
        o_rwkv, wkv_new_p, shift_new_p = _rwkv(proj, s_rwkv, shift_p, wkv_p, lp, l, row0=0, batch=bp, seq=seq,
                                               chunk=CHUNK, rows=MIXER_STEP_ROWS)
        s_gmlp, gv_s = _gmlp(proj, None, lp, l, row0=rows_p, batch=bs, seq=dseq, chunk=dseq, emit_v=True)
        (o_gmlp,) = _gmlp(proj, s_gmlp, lp, l, row0=0, batch=bp, seq=seq, chunk=GMLP_CHUNK, emit_v=False)
        s_conv, conv_new_s = _conv(proj, None, conv_s[l], lp, l, row0=rows_p, batch=bs, seq=dseq, tb=dseq)
        o_conv, conv_new_p = _conv(proj, s_conv, conv_p, lp, l, row0=0, batch=bp, seq=seq, tb=MIXER_STEP_ROWS)
        s_attn, k_new_s, v_new_s = _attn_sample(proj, cache_k, cache_v, bias, l, row0=rows_p, batch=bs, seq=dseq)
        o_attn, k_new_p, v_new_p = _attn_prompt(proj, s_attn, bias, l, batch=bp, seq=seq, keep=keep,
                                                rows=MIXER_STEP_ROWS)

        merged = _merge(h, w_gate, (o_rwkv, o_gmlp, o_conv, o_attn), w_branch_b, l)
        x = _resid(merged, w_out_b, x, l)
        act = _ffn_in(x, g_ffn, w_ffn_in_b, l)
        x = _resid(act, w_ffn_out_b, x, l)

        outs["p_shift"].append(shift_new_p[:, 0, :RWKV_COLS])
        outs["p_wkv"].append(wkv_new_p)
        outs["p_conv"].append(conv_new_p)
        outs["p_k"].append(k_new_p.reshape(bp, keep, HEADS, HEAD))
        outs["p_v"].append(v_new_p.reshape(bp, keep, HEADS, HEAD))
        outs["s_shift"].append(shift_new_s[:, 0, :RWKV_COLS])
        outs["s_wkv"].append(wkv_new_s)
        outs["s_conv"].append(conv_new_s)
        outs["s_k"].append(k_new_s.reshape(bs, dseq, HEADS, HEAD))
        outs["s_v"].append(v_new_s.reshape(bs, dseq, HEADS, HEAD))
        outs["s_gv"].append(gv_s.reshape(bs, dseq, MIX_W))

    g_final = norm_final.reshape(1, d)
    y_p = _final_norm(x, g_final, row0=0, rows=rows_p)
    y_s = _final_norm(x, g_final, row0=rows_p, rows=rows_s)
    st = {k: jnp.stack(v) for k, v in outs.items()}
    return (y_p.reshape(bp, seq, d), y_s.reshape(bs, dseq, d),
            st["p_shift"], st["p_wkv"], st["p_conv"], st["p_k"], st["p_v"],
            st["s_shift"], st["s_wkv"], st["s_conv"], st["s_k"], st["s_v"], st["s_gv"])
```

```python
import functools

import jax
import jax.numpy as jnp
from jax import lax
from jax.experimental import pallas as pl
from jax.experimental.pallas import tpu as pltpu

F32 = jnp.float32
BF16 = jnp.bfloat16

MIX_W = 512
N_BRANCH = 4
HEADS = 8
HEAD = 64
W_RANK = 96
A_RANK = 96
G_RANK = 256
RWKV_COLS = 3 * MIX_W + W_RANK + A_RANK + G_RANK
RWKV_PAD_COLS = 4 * MIX_W
GATE_ROW0 = RWKV_COLS + 7 * MIX_W
GN_EPS = 64e-5
RMS_EPS = 1e-6
LN_EPS = 1e-5
CHUNK = 64
GMLP_CHUNK = 128
CONV_W = 31
SUBLANES = 8
CONV_TAIL = 32
BAND_PAST = 8 * CHUNK
BAND = BAND_PAST + CHUNK
REL_CLIP = 128
REL_PAD = 384
SEQ_COLS = RWKV_PAD_COLS + 2 * MIX_W + 2 * MIX_W
ATTN_COLS = 3 * MIX_W
ATTN_ROW0 = RWKV_COLS + 4 * MIX_W
COL_GMLP = RWKV_PAD_COLS // (2 * MIX_W)
COL_CONV = COL_GMLP + 1

VMEM_LIMIT_BYTES = 56 * 1024 * 1024
ROW_TILE_SMALL = 768
ROW_TILE_LARGE = 1408
COL_TILE_TARGET = 512
MIXER_STEP_ROWS = 2 * CHUNK
MXU_TILE = 256


def _pick_tile(n, target, mult):
    best = None
    for t in range(mult, min(n, target) + 1, mult):
        if n % t == 0:
            best = t
    assert best is not None, (n, target, mult)
    return best


def _params(sem):
    return pltpu.CompilerParams(dimension_semantics=sem, vmem_limit_bytes=VMEM_LIMIT_BYTES)


def _rms_bf16(x, g):
    ms = jnp.mean(x * x, axis=-1, keepdims=True)
    return (x * lax.rsqrt(ms + RMS_EPS) * g).astype(BF16)


def _layer_norm(x, w, b):
    mu = jnp.mean(x, axis=-1, keepdims=True)
    var = jnp.mean(jnp.square(x - mu), axis=-1, keepdims=True)
    return (x - mu) * lax.rsqrt(var + LN_EPS) * w + b


def _gelu_tanh(x):
    return 0.5 * x * (1.0 + jnp.tanh(0.7978845608028654 * (x + 0.044715 * x * x * x)))


def _dot(a, b):
    return jnp.dot(a, b, preferred_element_type=F32)


def _dot_nt(a, b):
    return lax.dot_general(a, b, (((1,), (1,)), ((), ())), preferred_element_type=F32)


def _dot_tn(a, b):
    return lax.dot_general(a, b, (((0,), (0,)), ((), ())), preferred_element_type=F32)


def _inproj_kernel(x_ref, g_ref, w_ref, proj_ref, h_ref):
    @pl.when(pl.program_id(1) == 0)
    def _():
        h_ref[...] = _rms_bf16(x_ref[...], g_ref[...])

    proj_ref[...] = _dot_nt(h_ref[...], w_ref[0])


def _attn_proj_kernel(h_ref, w_ref, o_ref):
    o_ref[...] = _dot_nt(h_ref[...], w_ref[0])


def _attn_proj(h, w_in_t, layer):
    m, d = h.shape
    tm = _pick_tile(m, ROW_TILE_LARGE, 16)
    return pl.pallas_call(
        _attn_proj_kernel,
        grid=(m // tm,),
        in_specs=[
            pl.BlockSpec((tm, d), lambda i: (i, 0)),
            pl.BlockSpec((pl.Element(1), pl.Element(ATTN_COLS), pl.Element(d)), lambda i: (layer, ATTN_ROW0, 0)),
        ],
        out_specs=pl.BlockSpec((tm, ATTN_COLS), lambda i: (i, 0)),
        out_shape=jax.ShapeDtypeStruct((m, ATTN_COLS), F32),
        compiler_params=_params(("parallel",)),
        name="attn_proj",
    )(h, w_in_t)


def _inproj(x, g, w_in_t, layer):
    m, d = x.shape
    n = SEQ_COLS
    tm = _pick_tile(m, ROW_TILE_SMALL, 16)
    tn = _pick_tile(n, RWKV_PAD_COLS, 128)
    assert RWKV_PAD_COLS % tn == 0
    n_rwkv = RWKV_PAD_COLS // tn
    overlap = RWKV_PAD_COLS - RWKV_COLS
    row_start = lambda j: pl.multiple_of(j * tn - jnp.where(j >= n_rwkv, overlap, 0), overlap)
    return pl.pallas_call(
        _inproj_kernel,
        grid=(m // tm, n // tn),
        in_specs=[
            pl.BlockSpec((tm, d), lambda i, j: (i, 0)),
            pl.BlockSpec((None, 1, d), lambda i, j: (layer, 0, 0)),
            pl.BlockSpec((pl.Element(1), pl.Element(tn), pl.Element(d)), lambda i, j: (layer, row_start(j), 0)),
        ],
        out_specs=[
            pl.BlockSpec((tm, tn), lambda i, j: (i, j)),
            pl.BlockSpec((tm, d), lambda i, j: (i, 0)),
        ],
        out_shape=[jax.ShapeDtypeStruct((m, n), F32), jax.ShapeDtypeStruct((m, d), BF16)],
        compiler_params=_params(("parallel", "arbitrary")),
        name="inproj",
    )(x, g, w_in_t)


def _merge_kernel(h_ref, wg0, wg1, wg2, wg3, b0, b1, b2, b3, wb_ref, o_ref):
    h = h_ref[...]
    acc = None
    for n, (wg, br) in enumerate(((wg0, b0), (wg1, b1), (wg2, b2), (wg3, b3))):
        gate = jax.nn.sigmoid(_dot_nt(h, wg[0]))
        term = gate * _dot(br[...], wb_ref[n])
        acc = term if acc is None else acc + term
    o_ref[...] = acc.astype(BF16)


def _merge(h, w_gate, branches, w_branch, layer):
    m, d = h.shape
    tm = _pick_tile(m, ROW_TILE_SMALL, 16)
    tn = _pick_tile(d, COL_TILE_TARGET, 128)
    nj = d // tn
    gate_specs = [pl.BlockSpec((pl.Element(1), pl.Element(tn), pl.Element(d)),
                               functools.partial(lambda i, j, n: (layer, pl.multiple_of(GATE_ROW0 + n * d + j * tn, 64), 0),
                                                 n=n))
                  for n in range(N_BRANCH)]
    br_specs = [pl.BlockSpec((tm, MIX_W), lambda i, j: (i, 0)) for _ in range(N_BRANCH)]
    return pl.pallas_call(
        _merge_kernel,
        grid=(m // tm, nj),
        in_specs=[pl.BlockSpec((tm, d), lambda i, j: (i, 0))] + gate_specs + br_specs
        + [pl.BlockSpec((None, N_BRANCH, MIX_W, tn), lambda i, j: (layer, 0, 0, j))],
        out_specs=pl.BlockSpec((tm, tn), lambda i, j: (i, j)),
        out_shape=jax.ShapeDtypeStruct((m, d), BF16),
        compiler_params=_params(("parallel", "arbitrary")),
        name="merge",
    )(h, w_gate, w_gate, w_gate, w_gate, *branches, w_branch)


def _resid_kernel(a_ref, w_ref, x_ref, o_ref):
    o_ref[...] = x_ref[...] + _dot(a_ref[...], w_ref[...])


def _resid(a, w, x, layer):
    m, k = a.shape
    d = x.shape[-1]
    tm = _pick_tile(m, ROW_TILE_LARGE if 8 * ROW_TILE_LARGE * k * 2 <= VMEM_LIMIT_BYTES else ROW_TILE_SMALL, 16)
    blocks = lambda tn: 2 * (tm * k * 2 + k * tn * 2 + 2 * tm * tn * 4)
    tn = max(t for t in (_pick_tile(d, COL_TILE_TARGET, 128), _pick_tile(d, 2 * COL_TILE_TARGET, 128))
             if t <= COL_TILE_TARGET or blocks(t) <= VMEM_LIMIT_BYTES)
    return pl.pallas_call(
        _resid_kernel,
        grid=(m // tm, d // tn),
        in_specs=[
            pl.BlockSpec((tm, k), lambda i, j: (i, 0)),
            pl.BlockSpec((None, k, tn), lambda i, j: (layer, 0, j)),
            pl.BlockSpec((tm, tn), lambda i, j: (i, j)),
        ],
        out_specs=pl.BlockSpec((tm, tn), lambda i, j: (i, j)),
        out_shape=jax.ShapeDtypeStruct((m, d), F32),
        compiler_params=_params(("parallel", "arbitrary")),
        name="resid",
    )(a, w, x)


def _ffn_in_kernel(x_ref, g_ref, wg_ref, wu_ref, act_ref, h_scr):
    @pl.when(pl.program_id(1) == 0)
    def _():
        h_scr[...] = _rms_bf16(x_ref[...], g_ref[...])

    h = h_scr[...]
    act_ref[...] = (jax.nn.silu(_dot(h, wg_ref[...])) * _dot(h, wu_ref[...])).astype(BF16)


def _ffn_in(x, g, w_ffn_in, layer):
    m, d = x.shape
    dff = w_ffn_in.shape[-1] // 2
    tm = _pick_tile(m, ROW_TILE_LARGE, 16)
    tn = _pick_tile(dff, COL_TILE_TARGET, 128)
    nj = dff // tn
    return pl.pallas_call(
        _ffn_in_kernel,
        grid=(m // tm, nj),
        in_specs=[
            pl.BlockSpec((tm, d), lambda i, j: (i, 0)),
            pl.BlockSpec((None, 1, d), lambda i, j: (layer, 0, 0)),
            pl.BlockSpec((None, d, tn), lambda i, j: (layer, 0, j)),
            pl.BlockSpec((None, d, tn), lambda i, j: (layer, 0, nj + j)),
        ],
        out_specs=pl.BlockSpec((tm, tn), lambda i, j: (i, j)),
        out_shape=jax.ShapeDtypeStruct((m, dff), BF16),
        scratch_shapes=[pltpu.VMEM((tm, d), BF16)],
        compiler_params=_params(("parallel", "arbitrary")),
        name="ffn_in",
    )(x, g, w_ffn_in, w_ffn_in)


def _final_norm_kernel(x_ref, g_ref, o_ref):
    x = x_ref[...]
    ms = jnp.mean(x * x, axis=-1, keepdims=True)
    o_ref[...] = x * lax.rsqrt(ms + RMS_EPS) * g_ref[...]


def _final_norm(x, g, *, row0, rows):
    d = x.shape[-1]
    tm = _pick_tile(rows, ROW_TILE_SMALL, 16)
    assert row0 % tm == 0
    rb0 = row0 // tm
    return pl.pallas_call(
        _final_norm_kernel,
        grid=(rows // tm,),
        in_specs=[pl.BlockSpec((tm, d), lambda i: (rb0 + i, 0)), pl.BlockSpec((1, d), lambda i: (0, 0))],
        out_specs=pl.BlockSpec((tm, d), lambda i: (i, 0)),
        out_shape=jax.ShapeDtypeStruct((rows, d), F32),
        compiler_params=_params(("parallel",)),
        name="final_norm",
    )(x, g)


def _split3(x):
    hi = x.astype(BF16)
    r1 = x - hi.astype(F32)
    mid = r1.astype(BF16)
    lo = (r1 - mid.astype(F32)).astype(BF16)
    return hi, mid, lo


def _head_sum(x, ones_bd):
    rows = x.shape[0]
    w = ones_bd.shape[0]
    hi = x.astype(BF16)
    lo = (x - hi.astype(F32)).astype(BF16)
    st = jnp.concatenate([hi, lo], axis=0)
    s = jnp.concatenate([_dot(st[:, j:j + w], ones_bd) for j in range(0, x.shape[1], w)], axis=1)
    return s[:rows] + s[rows:]


def _with_tail(main, o_ref, tail_ref, n_main):
    i = pl.program_id(0)
    if tail_ref is None:
        main(i)
        return

    @pl.when(i < n_main)
    def _():
        main(i)

    @pl.when(i == n_main)
    def _():
        o_ref[...] = tail_ref[...]


def _rwkv_kernel(p_ref, shift0_ref, s0_ref, mu_ref, w0_ref, w2_ref, a0_ref, a2_ref, g2_ref, kkw_ref, ka_ref,
                 rk_ref, gnw_ref, gnb_ref, ones_ref, tri_ref, *rest, nc, L, n_main, has_tail):
    tail_ref = rest[0] if has_tail else None
    o_ref, sout_ref, shift_out_ref, carry_ref, s_ref, y_ref = rest[1:] if has_tail else rest
    main = functools.partial(_rwkv_step, p_ref, shift0_ref, s0_ref, mu_ref, w0_ref, w2_ref, a0_ref, a2_ref, g2_ref,
                             kkw_ref, ka_ref, rk_ref, gnw_ref, gnb_ref, ones_ref, tri_ref, o_ref, sout_ref,
                             shift_out_ref, carry_ref, s_ref, y_ref, nc=nc, L=L)
    _with_tail(main, o_ref, tail_ref, n_main)


def _rwkv_step(p_ref, shift0_ref, s0_ref, mu_ref, w0_ref, w2_ref, a0_ref, a2_ref, g2_ref, kkw_ref, ka_ref,
               rk_ref, gnw_ref, gnb_ref, ones_ref, tri_ref, o_ref, sout_ref, shift_out_ref,
               carry_ref, s_ref, y_ref, step, *, nc, L):
    rows = p_ref.shape[0]
    ncb = rows // L
    c = lax.rem(step, nc)

    @pl.when(c == 0)
    def _():
        carry_ref[...] = shift0_ref[...]
        s_ref[...] = s0_ref[...]

    p = p_ref[...]
    row = lax.broadcasted_iota(jnp.int32, p.shape, 0)
    prev = jnp.where(row == 0, carry_ref[...], pltpu.roll(p, 1, 0))
    carry_ref[...] = p[rows - 1:rows, :]
    xs = p + mu_ref[...] * (prev - p)

    o_w = 3 * MIX_W
    r = xs[:, 0:MIX_W]
    k = xs[:, MIX_W:2 * MIX_W]
    v = xs[:, 2 * MIX_W:3 * MIX_W]
    wl = xs[:, o_w:o_w + w2_ref.shape[0]]
    al = xs[:, o_w:o_w + a2_ref.shape[0]]
    gl = xs[:, o_w + MIX_W - g2_ref.shape[0]:o_w + MIX_W]
    ones_bd = ones_ref[...]

    zz = w0_ref[...] + _dot(jnp.tanh(wl).astype(BF16), w2_ref[...])
    nz = -zz
    softplus = jnp.maximum(nz, 0.0) + jnp.log(1.0 + jnp.exp(-jnp.abs(nz)))
    lw = -jnp.exp(-softplus - 0.5)
    a = jax.nn.sigmoid(a0_ref[...] + _dot(al.astype(BF16), a2_ref[...]))
    g = _dot(jax.nn.sigmoid(gl).astype(BF16), g2_ref[...])
    kk = k * kkw_ref[...]
    kk = kk * lax.rsqrt(jnp.maximum(_head_sum(kk * kk, ones_bd), 1e-24))
    k = k * (1.0 + (a - 1.0) * ka_ref[...])
    a_s = -kk
    b_s = kk * a
    bonus = _head_sum(r * k * rk_ref[...], ones_bd) * v

    cum3 = _dot(tri_ref[...], jnp.concatenate(_split3(lw), axis=1))
    cum = cum3[:, :MIX_W] + cum3[:, MIX_W:2 * MIX_W] + cum3[:, 2 * MIX_W:]
    ends = [cum[(ch + 1) * L - 1:(ch + 1) * L, :] for ch in range(ncb)]
    cl = jnp.concatenate([jnp.broadcast_to(e, (L, MIX_W)) for e in ends], axis=0) if ncb > 1 else ends[0]
    inv_p = jnp.exp(-cum)
    to_end = jnp.exp(cl - cum)
    at_f = a_s * jnp.exp(cum - lw)
    at = at_f.astype(BF16)
    bt = (b_s * inv_p).astype(BF16)
    kt = (k * inv_p).astype(BF16)
    rt = (r * jnp.exp(cum)).astype(BF16)
    bh = (b_s * to_end).astype(BF16)
    kh = (k * to_end).astype(BF16)
    p_end = [jnp.exp(e) for e in ends]
    vb = v.astype(BF16)

    ri = lax.broadcasted_iota(jnp.int32, (L, 2 * L), 0)
    ci = lax.broadcasted_iota(jnp.int32, (L, 2 * L), 1)
    ci = jnp.where(ci >= L, ci - L, ci)
    strict = ci < ri
    incl = ci <= ri

    chains = [(ch, h) for ch in range(ncb) for h in range(HEADS)]
    sl = lambda arr, ch, h: arr[ch * L:(ch + 1) * L, h * HEAD:(h + 1) * HEAD]
    gm = [_dot_nt(jnp.concatenate([sl(at, ch, h), sl(rt, ch, h)], axis=0),
                  jnp.concatenate([sl(kt, ch, h), sl(bt, ch, h)], axis=0)) for ch, h in chains]
    a_top = [jnp.where(strict, m[:L], 0.0) for m in gm]
    m_low = [jnp.where(incl, m[L:], 0.0).astype(BF16) for m in gm]
    q = [t[:, L:] for t in a_top]
    x = [jnp.concatenate([t[:, :L], sl(at_f, ch, h)], axis=1) for t, (ch, h) in zip(a_top, chains)]
    n_stage = L.bit_length() - 1
    for i in range(n_stage):
        last = i == n_stage - 1
        qb = [t.astype(BF16) for t in q]
        rhs = [t.astype(BF16) for t in x] if last else \
              [jnp.concatenate([t.astype(BF16), u], axis=1) for t, u in zip(x, qb)]
        res = [_dot(u, t) for u, t in zip(qb, rhs)]
        x = [t + u[:, :L + HEAD] for t, u in zip(x, res)]
        if not last:
            q = [u[:, L + HEAD:] for u in res]
    u_free = [_dot(t[:, :L].astype(BF16), sl(vb, ch, h)) for t, (ch, h) in zip(x, chains)]
    wm_rt = [jnp.concatenate([t[:, L:].astype(BF16), sl(rt, ch, h)], axis=0) for t, (ch, h) in zip(x, chains)]
    kb = [jnp.concatenate([sl(kh, ch, h), sl(bh, ch, h)], axis=0) for ch, h in chains]

    state = [s_ref[h] for h in range(HEADS)]
    for ch in range(ncb):
        ids = [ch * HEADS + h for h in range(HEADS)]
        ws = [_dot_nt(wm_rt[i], state[h].astype(BF16)) for h, i in enumerate(ids)]
        uv = [jnp.concatenate([sl(vb, ch, h), (ws[h][:L] + u_free[i]).astype(BF16)], axis=0)
              for h, i in enumerate(ids)]
        for h, i in enumerate(ids):
            y_ref[ch * L:(ch + 1) * L, h * HEAD:(h + 1) * HEAD] = ws[h][L:] + _dot(m_low[i], uv[h])
        state = [state[h] * p_end[ch][:, h * HEAD:(h + 1) * HEAD] + _dot_tn(uv[h], kb[i])
                 for h, i in enumerate(ids)]
    for h in range(HEADS):
        s_ref[h] = state[h]

    y = y_ref[...]
    mu_y = _head_sum(y, ones_bd) * (1.0 / HEAD)
    yc = y - mu_y
    var_y = _head_sum(yc * yc, ones_bd) * (1.0 / HEAD)
    yn = yc * lax.rsqrt(var_y + GN_EPS) * gnw_ref[...] + gnb_ref[...]
    o_ref[...] = ((yn + bonus) * g).astype(BF16)

    @pl.when(c == nc - 1)
    def _():
        sout_ref[...] = s_ref[...]
        shift_out_ref[...] = p[rows - 1:rows, :]


def _tail_plan(n_main, rows, tail):
    if tail is None:
        return n_main, (lambda i: i), [], []
    assert tail.shape == (rows, MIX_W), (tail.shape, rows)
    return (n_main + 1, (lambda i: jnp.minimum(i, n_main - 1)),
            [pl.BlockSpec((rows, MIX_W), lambda i: (0, 0))], [tail])


def _rwkv(proj, tail, shift0, s0, lp, layer, *, row0, batch, seq, chunk, rows):
    nc = seq // rows
    rb0 = row0 // rows
    n_main = batch * nc
    n_steps, clamp, tail_specs, tail_args = _tail_plan(n_main, rows, tail)
    seq_of = lambda i: clamp(i) // nc
    vec = lambda name: pl.BlockSpec((None, 1, lp[name].shape[-1]), lambda i: (layer, 0, 0))
    mat = lambda name: pl.BlockSpec((None,) + lp[name].shape[1:], lambda i: (layer, 0, 0))
    const = lambda arr: pl.BlockSpec(arr.shape, lambda i: (0,) * arr.ndim)
    tri = jnp.kron(jnp.eye(rows // chunk, dtype=F32), jnp.tril(jnp.ones((chunk, chunk), F32))).astype(BF16)
    ones_bd = lp["ones_bd"]
    in_specs = [
        pl.BlockSpec((rows, RWKV_PAD_COLS), lambda i: (rb0 + clamp(i), 0)),
        pl.BlockSpec((None, 1, RWKV_PAD_COLS), lambda i: (seq_of(i), 0, 0)),
        pl.BlockSpec((None, HEADS, HEAD, HEAD), lambda i: (seq_of(i), 0, 0, 0)),
        vec("rwkv_mu"), vec("rwkv_w0"), mat("rwkv_w2"), vec("rwkv_a0"), mat("rwkv_a2"), mat("rwkv_g2"),
        vec("rwkv_kk"), vec("rwkv_ka"), vec("rwkv_rk"), vec("rwkv_gn_w"), vec("rwkv_gn_b"),
        const(ones_bd), const(tri),
    ] + tail_specs
    args = [proj, shift0, s0, lp["rwkv_mu"], lp["rwkv_w0"], lp["rwkv_w2"], lp["rwkv_a0"], lp["rwkv_a2"],
            lp["rwkv_g2"], lp["rwkv_kk"], lp["rwkv_ka"], lp["rwkv_rk"], lp["rwkv_gn_w"], lp["rwkv_gn_b"],
            ones_bd, tri] + tail_args
    return pl.pallas_call(
        functools.partial(_rwkv_kernel, nc=nc, L=chunk, n_main=n_main, has_tail=tail is not None),
        grid=(n_steps,),
        in_specs=in_specs,
        out_specs=[pl.BlockSpec((rows, MIX_W), lambda i: (i, 0)),
                   pl.BlockSpec((None, HEADS, HEAD, HEAD), lambda i: (seq_of(i), 0, 0, 0)),
                   pl.BlockSpec((None, 1, RWKV_PAD_COLS), lambda i: (seq_of(i), 0, 0))],
        out_shape=[jax.ShapeDtypeStruct((n_steps * rows, MIX_W), BF16),
                   jax.ShapeDtypeStruct((batch, HEADS, HEAD, HEAD), F32),
                   jax.ShapeDtypeStruct((batch, 1, RWKV_PAD_COLS), F32)],
        scratch_shapes=[pltpu.VMEM((1, RWKV_PAD_COLS), F32), pltpu.VMEM((HEADS, HEAD, HEAD), F32),
                        pltpu.VMEM((rows, MIX_W), F32)],
        compiler_params=_params(("arbitrary",)),
        name="rwkv",
    )(*args)


def _gmlp_kernel(p_ref, lnw_ref, lnb_ref, ws_ref, bias_ref, *rest, emit_v, n_main, has_tail):
    tail_ref = rest[0] if has_tail else None
    rest = rest[1:] if has_tail else rest
    main = functools.partial(_gmlp_step, p_ref, lnw_ref, lnb_ref, ws_ref, bias_ref, *rest, emit_v=emit_v)
    _with_tail(main, rest[0], tail_ref, n_main)


def _gmlp_step(p_ref, lnw_ref, lnb_ref, ws_ref, bias_ref, o_ref, *rest, emit_v):
    if emit_v:
        v_ref, wm_ref, step = rest
    else:
        wm_ref, step = rest
    L = p_ref.shape[0]

    @pl.when(step == 0)
    def _():
        lower = lax.broadcasted_iota(jnp.int32, (L, L), 1) <= lax.broadcasted_iota(jnp.int32, (L, L), 0)
        for g in range(HEADS):
            wm_ref[g] = jnp.where(lower, ws_ref[g, :L, :L], 0.0).astype(BF16)

    p = p_ref[...]
    u = _gelu_tanh(p[:, :MIX_W])
    v = _layer_norm(_gelu_tanh(p[:, MIX_W:]), lnw_ref[...], lnb_ref[...])
    if emit_v:
        v_ref[...] = v
    vb = v.astype(BF16)
    pair_w = 2 * HEAD
    first = lax.broadcasted_iota(jnp.int32, (L, pair_w), 1) < HEAD
    pairs = [vb[:, pr * pair_w:(pr + 1) * pair_w] for pr in range(HEADS // 2)]
    mixed = [_dot(wm_ref[g], pairs[g // 2]) for g in range(HEADS)]
    mixed = jnp.concatenate([jnp.where(first, mixed[2 * pr], mixed[2 * pr + 1]) for pr in range(HEADS // 2)], axis=1)
    o_ref[...] = (u * (mixed + bias_ref[...])).astype(BF16)


def _gmlp(proj, tail, lp, layer, *, row0, batch, seq, chunk, emit_v):
    rb0 = row0 // chunk
    n_main = batch * seq // chunk
    n_steps, clamp, tail_specs, tail_args = _tail_plan(n_main, chunk, tail)
    bias = lp["gmlp_bias"][:, :chunk, :]
    in_specs = [
        pl.BlockSpec((chunk, 2 * MIX_W), lambda i: (rb0 + clamp(i), COL_GMLP)),
        pl.BlockSpec((None, 1, MIX_W), lambda i: (layer, 0, 0)),
        pl.BlockSpec((None, 1, MIX_W), lambda i: (layer, 0, 0)),
        pl.BlockSpec((None, HEADS, GMLP_CHUNK, GMLP_CHUNK), lambda i: (layer, 0, 0, 0)),
        pl.BlockSpec((None, chunk, MIX_W), lambda i: (layer, 0, 0)),
    ] + tail_specs
    args = [proj, lp["gmlp_ln_w"], lp["gmlp_ln_b"], lp["gmlp_ws"], bias] + tail_args
    out_specs = [pl.BlockSpec((chunk, MIX_W), lambda i: (i, 0))]
    out_shape = [jax.ShapeDtypeStruct((n_steps * chunk, MIX_W), BF16)]
    if emit_v:
        out_specs.append(pl.BlockSpec((chunk, MIX_W), lambda i: (clamp(i), 0)))
        out_shape.append(jax.ShapeDtypeStruct((n_main * chunk, MIX_W), F32))
    return pl.pallas_call(
        functools.partial(_gmlp_kernel, emit_v=emit_v, n_main=n_main, has_tail=tail is not None),
        grid=(n_steps,),
        in_specs=in_specs,
        out_specs=out_specs,
        out_shape=out_shape,
        scratch_shapes=[pltpu.VMEM((HEADS, chunk, chunk), BF16)],
        compiler_params=_params(("arbitrary",)),
        name="gmlp",
    )(*args)


def _conv_kernel(p_ref, prev_ref, dw_ref, dwb_ref, lnw_ref, lnb_ref, *rest, nt, n_main, has_tail):
    tail_ref = rest[0] if has_tail else None
    rest = rest[1:] if has_tail else rest
    main = functools.partial(_conv_step, p_ref, prev_ref, dw_ref, dwb_ref, lnw_ref, lnb_ref, *rest, nt=nt)
    _with_tail(main, rest[0], tail_ref, n_main)


def _conv_step(p_ref, prev_ref, dw_ref, dwb_ref, lnw_ref, lnb_ref, o_ref, state_ref, z_ref, step, *, nt):
    tb = p_ref.shape[0]
    t = lax.rem(step, nt)
    off = CONV_TAIL - (CONV_W - 1)

    @pl.when(t == 0)
    def _():
        z_ref[0, 0:CONV_TAIL, :] = prev_ref[...]

    @pl.when(t > 0)
    def _():
        z_ref[0, 0:CONV_TAIL, :] = z_ref[0, tb:tb + CONV_TAIL, :]

    p = p_ref[...]
    z_ref[0, CONV_TAIL:CONV_TAIL + tb, :] = p[:, :MIX_W] * jax.nn.sigmoid(p[:, MIX_W:])
    span = tb + CONV_TAIL - SUBLANES
    for s in range(1, SUBLANES):
        z_ref[s, 0:span, :] = z_ref[0, s:s + span, :]
    sub = min(tb, 32)
    for r0 in range(0, tb, sub):
        acc = None
        for w in range(CONV_W):
            a, s = divmod(w + off, SUBLANES)
            term = z_ref[s, r0 + a * SUBLANES:r0 + a * SUBLANES + sub, :] * dw_ref[w:w + 1, :]
            acc = term if acc is None else acc + term
        y = _layer_norm(acc + dwb_ref[...], lnw_ref[...], lnb_ref[...])
        o_ref[r0:r0 + sub, :] = jax.nn.silu(y).astype(BF16)

    @pl.when(t == nt - 1)
    def _():
        state_ref[...] = z_ref[0, tb + off:tb + CONV_TAIL, :]


def _conv(proj, tail, prev, lp, layer, *, row0, batch, seq, tb):
    nt = seq // tb
    rb0 = row0 // tb
    n_main = batch * nt
    n_steps, clamp, tail_specs, tail_args = _tail_plan(n_main, tb, tail)
    seq_of = lambda i: clamp(i) // nt
    in_specs = [
        pl.BlockSpec((tb, 2 * MIX_W), lambda i: (rb0 + clamp(i), COL_CONV)),
        pl.BlockSpec((None, CONV_TAIL, MIX_W), lambda i: (seq_of(i), 0, 0)),
        pl.BlockSpec((None, CONV_W, MIX_W), lambda i: (layer, 0, 0)),
        pl.BlockSpec((None, 1, MIX_W), lambda i: (layer, 0, 0)),
        pl.BlockSpec((None, 1, MIX_W), lambda i: (layer, 0, 0)),
        pl.BlockSpec((None, 1, MIX_W), lambda i: (layer, 0, 0)),
    ] + tail_specs
    args = [proj, prev, lp["conv_dw"], lp["conv_dw_b"], lp["conv_ln_w"], lp["conv_ln_b"]] + tail_args
    return pl.pallas_call(
        functools.partial(_conv_kernel, nt=nt, n_main=n_main, has_tail=tail is not None),
        grid=(n_steps,),
        in_specs=in_specs,
        out_specs=[pl.BlockSpec((tb, MIX_W), lambda i: (i, 0)),
                   pl.BlockSpec((None, CONV_W - 1, MIX_W), lambda i: (seq_of(i), 0, 0))],
        out_shape=[jax.ShapeDtypeStruct((n_steps * tb, MIX_W), BF16),
                   jax.ShapeDtypeStruct((batch, CONV_W - 1, MIX_W), F32)],
        scratch_shapes=[pltpu.VMEM((SUBLANES, tb + CONV_TAIL, MIX_W), F32)],
        compiler_params=_params(("arbitrary",)),
        name="conv",
    )(*args)


def _rel_bias_kernel(table_ref, o_ref):
    width = BAND + CHUNK
    u = lax.broadcasted_iota(jnp.int32, (REL_PAD, width), 1)
    m = lax.broadcasted_iota(jnp.int32, (REL_PAD, width), 0)
    idx = jnp.clip(BAND - 1 - u, -REL_CLIP, REL_CLIP) + REL_CLIP
    onehot = (idx == m).astype(F32)
    ext = jnp.dot(table_ref[...], onehot, preferred_element_type=F32, precision=lax.Precision.HIGHEST)
    for i in range(CHUNK):
        o_ref[i] = ext[:, CHUNK - 1 - i:CHUNK - 1 - i + BAND]


def _rel_bias(table_pad):
    depth = table_pad.shape[0]
    return pl.pallas_call(
        _rel_bias_kernel,
        grid=(depth,),
        in_specs=[pl.BlockSpec((None, HEADS, REL_PAD), lambda l: (l, 0, 0))],
        out_specs=pl.BlockSpec((None, CHUNK, HEADS, BAND), lambda l: (l, 0, 0, 0)),
        out_shape=jax.ShapeDtypeStruct((depth, CHUNK, HEADS, BAND), F32),
        compiler_params=_params(("parallel",)),
        name="rel_bias",
    )(table_pad)


def _attend_heads(q, k, v, bias_of, valid):
    assert HEAD ** -0.5 == 2.0 ** -3
    q = (q * (HEAD ** -0.5)).astype(BF16)
    rq = q.shape[0]
    pair_w = 2 * HEAD
    first = lax.broadcasted_iota(jnp.int32, (rq, pair_w), 1) < HEAD
    zero = jnp.zeros((rq, pair_w), q.dtype)
    units = []
    for pr in range(HEADS // 2):
        ps = slice(pr * pair_w, (pr + 1) * pair_w)
        for half in range(2):
            units.append((2 * pr + half, jnp.where(first if half == 0 else ~first, q[:, ps], zero), k[:, ps], v[:, ps]))
    s = [_dot_nt(qm, k2) + bias_of(h) for h, qm, k2, _ in units]
    if valid is not None:
        s = [jnp.where(valid, t, -1e30) for t in s]
    e = [jnp.exp(t - jnp.max(t, axis=-1, keepdims=True)) for t in s]
    pr_ = [(t * (1.0 / jnp.sum(t, axis=-1, keepdims=True))).astype(BF16) for t in e]
    o = [_dot(t, v2) for t, (_, _, _, v2) in zip(pr_, units)]
    return jnp.concatenate([jnp.where(first, o[2 * pr], o[2 * pr + 1]) for pr in range(HEADS // 2)], axis=1)


def _attn_prompt_kernel(q_ref, k_ref, v_ref, bias_ref, *rest, nq, n_main, has_tail):
    tail_ref = rest[0] if has_tail else None
    rest = rest[1:] if has_tail else rest
    main = functools.partial(_attn_prompt_step, q_ref, k_ref, v_ref, bias_ref, *rest, nq=nq)
    _with_tail(main, rest[0], tail_ref, n_main)


def _attn_prompt_step(q_ref, k_ref, v_ref, bias_ref, o_ref, knew_ref, vnew_ref, kp_ref, vp_ref, step, *, nq):
    rows = q_ref.shape[0]
    seq = k_ref.shape[0]
    keep = knew_ref.shape[0]
    c = lax.rem(step, nq)

    @pl.when(c == 0)
    def _():
        kp_ref[0:BAND_PAST, :] = jnp.zeros((BAND_PAST, MIX_W), BF16)
        vp_ref[0:BAND_PAST, :] = jnp.zeros((BAND_PAST, MIX_W), BF16)
        kp_ref[BAND_PAST:, :] = k_ref[...].astype(BF16)
        vp_ref[BAND_PAST:, :] = v_ref[...].astype(BF16)
        knew_ref[...] = k_ref[seq - keep:, :]
        vnew_ref[...] = v_ref[seq - keep:, :]

    def attend(masked):
        col = lax.broadcasted_iota(jnp.int32, (CHUNK, BAND), 1)
        outs = []
        for cb in range(rows // CHUNK):
            r0 = pl.multiple_of(c * rows + cb * CHUNK, CHUNK)
            outs.append(_attend_heads(q_ref[cb * CHUNK:(cb + 1) * CHUNK, :], kp_ref[pl.ds(r0, BAND), :],
                                      vp_ref[pl.ds(r0, BAND), :], lambda h: bias_ref[h],
                                      (r0 + col) >= BAND_PAST if masked else None))
        o_ref[...] = jnp.concatenate(outs, axis=0).astype(BF16)

    reaches_start = c * rows < BAND_PAST
    pl.when(reaches_start)(lambda: attend(True))
    pl.when(jnp.logical_not(reaches_start))(lambda: attend(False))


def _attn_prompt(proj, tail, bias, layer, *, batch, seq, keep, rows):
    nq = seq // rows
    n_main = batch * nq
    n_steps, clamp, tail_specs, tail_args = _tail_plan(n_main, rows, tail)
    seq_of = lambda i: clamp(i) // nq
    return pl.pallas_call(
        functools.partial(_attn_prompt_kernel, nq=nq, n_main=n_main, has_tail=tail is not None),
        grid=(n_steps,),
        in_specs=[
            pl.BlockSpec((rows, MIX_W), lambda i: (clamp(i), 0)),
            pl.BlockSpec((seq, MIX_W), lambda i: (seq_of(i), 1)),
            pl.BlockSpec((seq, MIX_W), lambda i: (seq_of(i), 2)),
            pl.BlockSpec((None, HEADS, CHUNK, BAND), lambda i: (layer, 0, 0, 0)),
        ] + tail_specs,
        out_specs=[pl.BlockSpec((rows, MIX_W), lambda i: (i, 0)),
                   pl.BlockSpec((None, keep, MIX_W), lambda i: (seq_of(i), 0, 0)),
                   pl.BlockSpec((None, keep, MIX_W), lambda i: (seq_of(i), 0, 0))],
        out_shape=[jax.ShapeDtypeStruct((n_steps * rows, MIX_W), BF16),
                   jax.ShapeDtypeStruct((batch, keep, MIX_W), F32),
                   jax.ShapeDtypeStruct((batch, keep, MIX_W), F32)],
        scratch_shapes=[pltpu.VMEM((BAND_PAST + seq, MIX_W), BF16), pltpu.VMEM((BAND_PAST + seq, MIX_W), BF16)],
        compiler_params=_params(("arbitrary",)),
        name="attn_prompt",
    )(proj, proj, proj, bias, *tail_args)


def _attn_sample_kernel(q_ref, k_ref, v_ref, ck_ref, cv_ref, bias_ref, o_ref, knew_ref, vnew_ref, kp_ref, vp_ref):
    t = q_ref.shape[0]
    past = ck_ref.shape[0]
    kp_ref[0:past, :] = ck_ref[...].astype(BF16)
    vp_ref[0:past, :] = cv_ref[...].astype(BF16)
    kp_ref[past:, :] = k_ref[...].astype(BF16)
    vp_ref[past:, :] = v_ref[...].astype(BF16)
    knew_ref[...] = k_ref[...]
    vnew_ref[...] = v_ref[...]
    o = _attend_heads(q_ref[...], kp_ref[...], vp_ref[...], lambda h: bias_ref[h, :t, :past + t], None)
    o_ref[...] = o.astype(BF16)


def _attn_sample(proj, cache_k, cache_v, bias, layer, *, row0, batch, seq):
    rb0 = row0 // seq
    past = cache_k.shape[2]
    return pl.pallas_call(
        _attn_sample_kernel,
        grid=(batch,),
        in_specs=[
            pl.BlockSpec((seq, MIX_W), lambda b: (rb0 + b, 0)),
            pl.BlockSpec((seq, MIX_W), lambda b: (rb0 + b, 1)),
            pl.BlockSpec((seq, MIX_W), lambda b: (rb0 + b, 2)),
            pl.BlockSpec((None, None, past, MIX_W), lambda b: (layer, b, 0, 0)),
            pl.BlockSpec((None, None, past, MIX_W), lambda b: (layer, b, 0, 0)),
            pl.BlockSpec((None, HEADS, CHUNK, BAND), lambda b: (layer, 0, 0, 0)),
        ],
        out_specs=[pl.BlockSpec((seq, MIX_W), lambda b: (b, 0)),
                   pl.BlockSpec((None, seq, MIX_W), lambda b: (b, 0, 0)),
                   pl.BlockSpec((None, seq, MIX_W), lambda b: (b, 0, 0))],
        out_shape=[jax.ShapeDtypeStruct((batch * seq, MIX_W), BF16),
                   jax.ShapeDtypeStruct((batch, seq, MIX_W), F32),
                   jax.ShapeDtypeStruct((batch, seq, MIX_W), F32)],
        scratch_shapes=[pltpu.VMEM((past + seq, MIX_W), BF16), pltpu.VMEM((past + seq, MIX_W), BF16)],
        compiler_params=_params(("parallel",)),
        name="attn_sample",
    )(proj, proj, proj, cache_k, cache_v, bias)


def _pad_rwkv_cols(x):
    return jnp.pad(x, [(0, 0)] * (x.ndim - 1) + [(0, RWKV_PAD_COLS - RWKV_COLS)])


def _lora_rows(w, start, first, last):
    return jnp.pad(w, ((0, 0), (start - first, last - start - w.shape[1]), (0, 0))).astype(BF16)


def kernel(x_prompt, x_sample, state_rwkv_shift, state_rwkv_wkv, cache_conv, cache_attn_k, cache_attn_v,
           norm_mix, norm_ffn, norm_final, w_in, rwkv_mu, rwkv_w0, rwkv_w2, rwkv_a0, rwkv_a2, rwkv_g2,
           rwkv_kk, rwkv_ka, rwkv_rk, rwkv_gn_w, rwkv_gn_b, gmlp_ln_w, gmlp_ln_b, gmlp_ws, gmlp_bs,
           conv_dw, conv_dw_b, conv_ln_w, conv_ln_b, attn_rel_bias, w_branch, w_out, w_ffn_in, w_ffn_out):
    bp, seq, d = x_prompt.shape
    bs, dseq, _ = x_sample.shape
    depth = w_in.shape[0]
    rows_p = bp * seq
    rows_s = bs * dseq
    assert seq % GMLP_CHUNK == 0 and dseq <= CHUNK and rows_p % dseq == 0

    w_mix = w_gate = jnp.swapaxes(w_in, 1, 2).astype(BF16)
    w_branch_b = w_branch.astype(BF16)
    w_out_b = w_out.astype(BF16)
    w_ffn_in_b = w_ffn_in.astype(BF16)
    w_ffn_out_b = w_ffn_out.astype(BF16)
    row3 = lambda p: p.reshape(depth, 1, -1)
    lane_tile = 128
    lp = {
        "rwkv_mu": row3(_pad_rwkv_cols(rwkv_mu)), "rwkv_w0": row3(rwkv_w0), "rwkv_a0": row3(rwkv_a0),
        "rwkv_w2": _lora_rows(rwkv_w2, 0, 0, lane_tile),
        "rwkv_a2": _lora_rows(rwkv_a2, W_RANK, 0, 2 * lane_tile),
        "rwkv_g2": _lora_rows(rwkv_g2, W_RANK + A_RANK, lane_tile, MIX_W),
        "rwkv_kk": row3(rwkv_kk), "rwkv_ka": row3(rwkv_ka),
        "rwkv_rk": row3(rwkv_rk), "rwkv_gn_w": row3(rwkv_gn_w), "rwkv_gn_b": row3(rwkv_gn_b),
        "ones_bd": jnp.kron(jnp.eye(MXU_TILE // HEAD, dtype=F32), jnp.ones((HEAD, HEAD), F32)).astype(BF16),
        "gmlp_ln_w": row3(gmlp_ln_w), "gmlp_ln_b": row3(gmlp_ln_b), "gmlp_ws": gmlp_ws,
        "gmlp_bias": jnp.repeat(jnp.swapaxes(gmlp_bs, 1, 2), HEAD, axis=2),
        "conv_dw": conv_dw, "conv_dw_b": row3(conv_dw_b), "conv_ln_w": row3(conv_ln_w),
        "conv_ln_b": row3(conv_ln_b),
    }
    g_mix = row3(norm_mix)
    g_ffn = row3(norm_ffn)
    table_pad = jnp.pad(attn_rel_bias, ((0, 0), (0, 0), (0, REL_PAD - attn_rel_bias.shape[-1])))
    bias = jnp.swapaxes(_rel_bias(table_pad), 1, 2)

    past = cache_attn_k.shape[2]
    cache_k = cache_attn_k.reshape(depth, bs, past, MIX_W)
    cache_v = cache_attn_v.reshape(depth, bs, past, MIX_W)
    shift_s = _pad_rwkv_cols(state_rwkv_shift).reshape(depth, bs, 1, RWKV_PAD_COLS)
    conv_s = jnp.pad(cache_conv, ((0, 0), (0, 0), (CONV_TAIL - (CONV_W - 1), 0), (0, 0)))
    shift_p = jnp.zeros((bp, 1, RWKV_PAD_COLS), F32)
    wkv_p = jnp.zeros((bp, HEADS, HEAD, HEAD), F32)
    conv_p = jnp.zeros((bp, CONV_TAIL, MIX_W), F32)

    x = jnp.concatenate([x_prompt.reshape(rows_p, d), x_sample.reshape(rows_s, d)], axis=0)
    keep = min(BAND_PAST, seq)
    assert rows_s == MIXER_STEP_ROWS and seq % MIXER_STEP_ROWS == 0
    outs = {k: [] for k in ("p_shift", "p_wkv", "p_conv", "p_k", "p_v",
                            "s_shift", "s_wkv", "s_conv", "s_k", "s_v", "s_gv")}
    for l in range(depth):
        proj, h = _inproj(x, g_mix, w_mix, l)
        qkv = _attn_proj(h, w_mix, l)

        s_rwkv, wkv_new_s, shift_new_s = _rwkv(proj, None, shift_s[l], state_rwkv_wkv[l], lp, l, row0=rows_p,
                                               batch=bs, seq=dseq, chunk=dseq, rows=dseq)
        o_rwkv, wkv_new_p, shift_new_p = _rwkv(proj, s_rwkv, shift_p, wkv_p, lp, l, row0=0, batch=bp, seq=seq,
                                               chunk=CHUNK, rows=MIXER_STEP_ROWS)
        s_gmlp, gv_s = _gmlp(proj, None, lp, l, row0=rows_p, batch=bs, seq=dseq, chunk=dseq, emit_v=True)
        (o_gmlp,) = _gmlp(proj, s_gmlp, lp, l, row0=0, batch=bp, seq=seq, chunk=GMLP_CHUNK, emit_v=False)
        s_conv, conv_new_s = _conv(proj, None, conv_s[l], lp, l, row0=rows_p, batch=bs, seq=dseq, tb=dseq)
        o_conv, conv_new_p = _conv(proj, s_conv, conv_p, lp, l, row0=0, batch=bp, seq=seq, tb=MIXER_STEP_ROWS)
        s_attn, k_new_s, v_new_s = _attn_sample(qkv, cache_k, cache_v, bias, l, row0=rows_p, batch=bs, seq=dseq)
        o_attn, k_new_p, v_new_p = _attn_prompt(qkv, s_attn, bias, l, batch=bp, seq=seq, keep=keep,
                                                rows=MIXER_STEP_ROWS)

        merged = _merge(h, w_gate, (o_rwkv, o_gmlp, o_conv, o_attn), w_branch_b, l)
        x = _resid(merged, w_out_b, x, l)
        act = _ffn_in(x, g_ffn, w_ffn_in_b, l)
        x = _resid(act, w_ffn_out_b, x, l)

        outs["p_shift"].append(shift_new_p[:, 0, :RWKV_COLS])
        outs["p_wkv"].append(wkv_new_p)
        outs["p_conv"].append(conv_new_p)
        outs["p_k"].append(k_new_p.reshape(bp, keep, HEADS, HEAD))
        outs["p_v"].append(v_new_p.reshape(bp, keep, HEADS, HEAD))
        outs["s_shift"].append(shift_new_s[:, 0, :RWKV_COLS])
        outs["s_wkv"].append(wkv_new_s)
        outs["s_conv"].append(conv_new_s)
        outs["s_k"].append(k_new_s.reshape(bs, dseq, HEADS, HEAD))
        outs["s_v"].append(v_new_s.reshape(bs, dseq, HEADS, HEAD))
        outs["s_gv"].append(gv_s.reshape(bs, dseq, MIX_W))

    g_final = norm_final.reshape(1, d)
    y_p = _final_norm(x, g_final, row0=0, rows=rows_p)
    y_s = _final_norm(x, g_final, row0=rows_p, rows=rows_s)
    st = {k: jnp.stack(v) for k, v in outs.items()}
    return (y_p.reshape(bp, seq, d), y_s.reshape(bs, dseq, d),
            st["p_shift"], st["p_wkv"], st["p_conv"], st["p_k"], st["p_v"],
            st["s_shift"], st["s_wkv"], st["s_conv"], st["s_k"], st["s_v"], st["s_gv"])
```

```python
import functools

import jax
import jax.numpy as jnp
from jax import lax
from jax.experimental import pallas as pl
from jax.experimental.pallas import tpu as pltpu

F32 = jnp.float32
BF16 = jnp.bfloat16

MIX_W = 512
N_BRANCH = 4
HEADS = 8
HEAD = 64
W_RANK = 96
A_RANK = 96
G_RANK = 256
RWKV_COLS = 3 * MIX_W + W_RANK + A_RANK + G_RANK
RWKV_PAD_COLS = 4 * MIX_W
GATE_ROW0 = RWKV_COLS + 7 * MIX_W
GN_EPS = 64e-5
RMS_EPS = 1e-6
LN_EPS = 1e-5
CHUNK = 64
GMLP_CHUNK = 128
CONV_W = 31
SUBLANES = 8
CONV_TAIL = 32
BAND_PAST = 8 * CHUNK
BAND = BAND_PAST + CHUNK
REL_CLIP = 128
REL_PAD = 384
SEQ_COLS = RWKV_PAD_COLS + 2 * MIX_W + 2 * MIX_W
ATTN_COLS = 3 * MIX_W
ATTN_ROW0 = RWKV_COLS + 4 * MIX_W
COL_GMLP = RWKV_PAD_COLS // (2 * MIX_W)
COL_CONV = COL_GMLP + 1

VMEM_LIMIT_BYTES = 52 * 1024 * 1024
VMEM_LIMIT_MERGE_BYTES = 58 * 1024 * 1024
ROW_TILE_SMALL = 768
ROW_TILE_LARGE = 1408
COL_TILE_TARGET = 512
MIXER_STEP_ROWS = 2 * CHUNK
MXU_TILE = 256


def _pick_tile(n, target, mult):
    best = None
    for t in range(mult, min(n, target) + 1, mult):
        if n % t == 0:
            best = t
    assert best is not None, (n, target, mult)
    return best


def _params(sem, vmem_limit_bytes=VMEM_LIMIT_BYTES):
    return pltpu.CompilerParams(dimension_semantics=sem, vmem_limit_bytes=vmem_limit_bytes)


def _rms_bf16(x, g):
    ms = jnp.mean(x * x, axis=-1, keepdims=True)
    return (x * lax.rsqrt(ms + RMS_EPS) * g).astype(BF16)


def _layer_norm(x, w, b):
    mu = jnp.mean(x, axis=-1, keepdims=True)
    var = jnp.mean(jnp.square(x - mu), axis=-1, keepdims=True)
    return (x - mu) * lax.rsqrt(var + LN_EPS) * w + b


def _gelu_tanh(x):
    return 0.5 * x * (1.0 + jnp.tanh(0.7978845608028654 * (x + 0.044715 * x * x * x)))


def _dot(a, b):
    return jnp.dot(a, b, preferred_element_type=F32)


def _dot_nt(a, b):
    return lax.dot_general(a, b, (((1,), (1,)), ((), ())), preferred_element_type=F32)


def _dot_tn(a, b):
    return lax.dot_general(a, b, (((0,), (0,)), ((), ())), preferred_element_type=F32)


def _inproj_kernel(x_ref, g_ref, w_ref, proj_ref, h_ref):
    @pl.when(pl.program_id(1) == 0)
    def _():
        h_ref[...] = _rms_bf16(x_ref[...], g_ref[...])

    proj_ref[...] = _dot_nt(h_ref[...], w_ref[0])


def _attn_proj_kernel(h_ref, w_ref, o_ref):
    o_ref[...] = _dot_nt(h_ref[...], w_ref[0])


def _attn_proj(h, w_in_t, layer):
    m, d = h.shape
    tm = _pick_tile(m, ROW_TILE_LARGE, 16)
    return pl.pallas_call(
        _attn_proj_kernel,
        grid=(m // tm,),
        in_specs=[
            pl.BlockSpec((tm, d), lambda i: (i, 0)),
            pl.BlockSpec((pl.Element(1), pl.Element(ATTN_COLS), pl.Element(d)), lambda i: (layer, ATTN_ROW0, 0)),
        ],
        out_specs=pl.BlockSpec((tm, ATTN_COLS), lambda i: (i, 0)),
        out_shape=jax.ShapeDtypeStruct((m, ATTN_COLS), F32),
        compiler_params=_params(("parallel",)),
        name="attn_proj",
    )(h, w_in_t)


def _inproj(x, g, w_in_t, layer):
    m, d = x.shape
    n = SEQ_COLS
    tm = _pick_tile(m, ROW_TILE_SMALL, 16)
    tn = _pick_tile(n, RWKV_PAD_COLS, 128)
    assert RWKV_PAD_COLS % tn == 0
    n_rwkv = RWKV_PAD_COLS // tn
    overlap = RWKV_PAD_COLS - RWKV_COLS
    row_start = lambda j: pl.multiple_of(j * tn - jnp.where(j >= n_rwkv, overlap, 0), overlap)
    return pl.pallas_call(
        _inproj_kernel,
        grid=(m // tm, n // tn),
        in_specs=[
            pl.BlockSpec((tm, d), lambda i, j: (i, 0)),
            pl.BlockSpec((None, 1, d), lambda i, j: (layer, 0, 0)),
            pl.BlockSpec((pl.Element(1), pl.Element(tn), pl.Element(d)), lambda i, j: (layer, row_start(j), 0)),
        ],
        out_specs=[
            pl.BlockSpec((tm, tn), lambda i, j: (i, j)),
            pl.BlockSpec((tm, d), lambda i, j: (i, 0)),
        ],
        out_shape=[jax.ShapeDtypeStruct((m, n), F32), jax.ShapeDtypeStruct((m, d), BF16)],
        compiler_params=_params(("parallel", "arbitrary")),
        name="inproj",
    )(x, g, w_in_t)


def _merge_kernel(h_ref, wg0, wg1, wg2, wg3, b0, b1, b2, b3, wb_ref, o_ref):
    h = h_ref[...]
    acc = None
    for n, (wg, br) in enumerate(((wg0, b0), (wg1, b1), (wg2, b2), (wg3, b3))):
        gate = jax.nn.sigmoid(_dot_nt(h, wg[0]))
        term = gate * _dot(br[...], wb_ref[n])
        acc = term if acc is None else acc + term
    o_ref[...] = acc.astype(BF16)


def _merge(h, w_gate, branches, w_branch, layer):
    m, d = h.shape
    tm = _pick_tile(m, ROW_TILE_LARGE, 16)
    tn = _pick_tile(d, COL_TILE_TARGET, 128)
    nj = d // tn
    gate_specs = [pl.BlockSpec((pl.Element(1), pl.Element(tn), pl.Element(d)),
                               functools.partial(lambda i, j, n: (layer, pl.multiple_of(GATE_ROW0 + n * d + j * tn, 64), 0),
                                                 n=n))
                  for n in range(N_BRANCH)]
    br_specs = [pl.BlockSpec((tm, MIX_W), lambda i, j: (i, 0)) for _ in range(N_BRANCH)]
    return pl.pallas_call(
        _merge_kernel,
        grid=(m // tm, nj),
        in_specs=[pl.BlockSpec((tm, d), lambda i, j: (i, 0))] + gate_specs + br_specs
        + [pl.BlockSpec((None, N_BRANCH, MIX_W, tn), lambda i, j: (layer, 0, 0, j))],
        out_specs=pl.BlockSpec((tm, tn), lambda i, j: (i, j)),
        out_shape=jax.ShapeDtypeStruct((m, d), BF16),
        compiler_params=_params(("parallel", "arbitrary"), VMEM_LIMIT_MERGE_BYTES),
        name="merge",
    )(h, w_gate, w_gate, w_gate, w_gate, *branches, w_branch)


def _resid_kernel(a_ref, w_ref, x_ref, o_ref):
    o_ref[...] = x_ref[...] + _dot(a_ref[...], w_ref[...])


def _resid(a, w, x, layer):
    m, k = a.shape
    d = x.shape[-1]
    tm = _pick_tile(m, ROW_TILE_LARGE if 8 * ROW_TILE_LARGE * k * 2 <= VMEM_LIMIT_BYTES else ROW_TILE_SMALL, 16)
    blocks = lambda tn: 2 * (tm * k * 2 + k * tn * 2 + 2 * tm * tn * 4)
    tn = max(t for t in (_pick_tile(d, COL_TILE_TARGET, 128), _pick_tile(d, 2 * COL_TILE_TARGET, 128))
             if t <= COL_TILE_TARGET or blocks(t) <= VMEM_LIMIT_BYTES)
    return pl.pallas_call(
        _resid_kernel,
        grid=(m // tm, d // tn),
        in_specs=[
            pl.BlockSpec((tm, k), lambda i, j: (i, 0)),
            pl.BlockSpec((None, k, tn), lambda i, j: (layer, 0, j)),
            pl.BlockSpec((tm, tn), lambda i, j: (i, j)),
        ],
        out_specs=pl.BlockSpec((tm, tn), lambda i, j: (i, j)),
        out_shape=jax.ShapeDtypeStruct((m, d), F32),
        compiler_params=_params(("parallel", "arbitrary")),
        name="resid",
    )(a, w, x)


def _ffn_in_kernel(x_ref, g_ref, wg_ref, wu_ref, act_ref, h_scr):
    @pl.when(pl.program_id(1) == 0)
    def _():
        h_scr[...] = _rms_bf16(x_ref[...], g_ref[...])

    h = h_scr[...]
    act_ref[...] = (jax.nn.silu(_dot(h, wg_ref[...])) * _dot(h, wu_ref[...])).astype(BF16)


def _ffn_in(x, g, w_ffn_in, layer):
    m, d = x.shape
    dff = w_ffn_in.shape[-1] // 2
    tm = _pick_tile(m, ROW_TILE_LARGE, 16)
    tn = _pick_tile(dff, COL_TILE_TARGET, 128)
    nj = dff // tn
    return pl.pallas_call(
        _ffn_in_kernel,
        grid=(m // tm, nj),
        in_specs=[
            pl.BlockSpec((tm, d), lambda i, j: (i, 0)),
            pl.BlockSpec((None, 1, d), lambda i, j: (layer, 0, 0)),
            pl.BlockSpec((None, d, tn), lambda i, j: (layer, 0, j)),
            pl.BlockSpec((None, d, tn), lambda i, j: (layer, 0, nj + j)),
        ],
        out_specs=pl.BlockSpec((tm, tn), lambda i, j: (i, j)),
        out_shape=jax.ShapeDtypeStruct((m, dff), BF16),
        scratch_shapes=[pltpu.VMEM((tm, d), BF16)],
        compiler_params=_params(("parallel", "arbitrary")),
        name="ffn_in",
    )(x, g, w_ffn_in, w_ffn_in)


def _final_norm_kernel(x_ref, g_ref, o_ref):
    x = x_ref[...]
    ms = jnp.mean(x * x, axis=-1, keepdims=True)
    o_ref[...] = x * lax.rsqrt(ms + RMS_EPS) * g_ref[...]


def _final_norm(x, g, *, row0, rows):
    d = x.shape[-1]
    tm = _pick_tile(rows, ROW_TILE_SMALL, 16)
    assert row0 % tm == 0
    rb0 = row0 // tm
    return pl.pallas_call(
        _final_norm_kernel,
        grid=(rows // tm,),
        in_specs=[pl.BlockSpec((tm, d), lambda i: (rb0 + i, 0)), pl.BlockSpec((1, d), lambda i: (0, 0))],
        out_specs=pl.BlockSpec((tm, d), lambda i: (i, 0)),
        out_shape=jax.ShapeDtypeStruct((rows, d), F32),
        compiler_params=_params(("parallel",)),
        name="final_norm",
    )(x, g)


def _split3(x):
    hi = x.astype(BF16)
    r1 = x - hi.astype(F32)
    mid = r1.astype(BF16)
    lo = (r1 - mid.astype(F32)).astype(BF16)
    return hi, mid, lo


def _head_sum(x, ones_bd):
    rows = x.shape[0]
    w = ones_bd.shape[0]
    hi = x.astype(BF16)
    lo = (x - hi.astype(F32)).astype(BF16)
    st = jnp.concatenate([hi, lo], axis=0)
    s = jnp.concatenate([_dot(st[:, j:j + w], ones_bd) for j in range(0, x.shape[1], w)], axis=1)
    return s[:rows] + s[rows:]


def _with_tail(main, o_ref, tail_ref, n_main):
    i = pl.program_id(0)
    if tail_ref is None:
        main(i)
        return

    @pl.when(i < n_main)
    def _():
        main(i)

    @pl.when(i == n_main)
    def _():
        o_ref[...] = tail_ref[...]


def _rwkv_kernel(p_ref, shift0_ref, s0_ref, mu_ref, w0_ref, w2_ref, a0_ref, a2_ref, g2_ref, kkw_ref, ka_ref,
                 rk_ref, gnw_ref, gnb_ref, ones_ref, tri_ref, *rest, nc, L, n_main, has_tail):
    tail_ref = rest[0] if has_tail else None
    o_ref, sout_ref, shift_out_ref, carry_ref, s_ref, y_ref = rest[1:] if has_tail else rest
    main = functools.partial(_rwkv_step, p_ref, shift0_ref, s0_ref, mu_ref, w0_ref, w2_ref, a0_ref, a2_ref, g2_ref,
                             kkw_ref, ka_ref, rk_ref, gnw_ref, gnb_ref, ones_ref, tri_ref, o_ref, sout_ref,
                             shift_out_ref, carry_ref, s_ref, y_ref, nc=nc, L=L)
    _with_tail(main, o_ref, tail_ref, n_main)


def _rwkv_step(p_ref, shift0_ref, s0_ref, mu_ref, w0_ref, w2_ref, a0_ref, a2_ref, g2_ref, kkw_ref, ka_ref,
               rk_ref, gnw_ref, gnb_ref, ones_ref, tri_ref, o_ref, sout_ref, shift_out_ref,
               carry_ref, s_ref, y_ref, step, *, nc, L):
    rows = p_ref.shape[0]
    ncb = rows // L
    c = lax.rem(step, nc)

    @pl.when(c == 0)
    def _():
        carry_ref[...] = shift0_ref[...]
        s_ref[...] = s0_ref[...]

    p = p_ref[...]
    row = lax.broadcasted_iota(jnp.int32, p.shape, 0)
    prev = jnp.where(row == 0, carry_ref[...], pltpu.roll(p, 1, 0))
    carry_ref[...] = p[rows - 1:rows, :]
    xs = p + mu_ref[...] * (prev - p)

    o_w = 3 * MIX_W
    r = xs[:, 0:MIX_W]
    k = xs[:, MIX_W:2 * MIX_W]
    v = xs[:, 2 * MIX_W:3 * MIX_W]
    wl = xs[:, o_w:o_w + w2_ref.shape[0]]
    al = xs[:, o_w:o_w + a2_ref.shape[0]]
    gl = xs[:, o_w + MIX_W - g2_ref.shape[0]:o_w + MIX_W]
    ones_bd = ones_ref[...]

    zz = w0_ref[...] + _dot(jnp.tanh(wl).astype(BF16), w2_ref[...])
    nz = -zz
    softplus = jnp.maximum(nz, 0.0) + jnp.log(1.0 + jnp.exp(-jnp.abs(nz)))
    lw = -jnp.exp(-softplus - 0.5)
    a = jax.nn.sigmoid(a0_ref[...] + _dot(al.astype(BF16), a2_ref[...]))
    g = _dot(jax.nn.sigmoid(gl).astype(BF16), g2_ref[...])
    kk = k * kkw_ref[...]
    kk = kk * lax.rsqrt(jnp.maximum(_head_sum(kk * kk, ones_bd), 1e-24))
    k = k * (1.0 + (a - 1.0) * ka_ref[...])
    a_s = -kk
    b_s = kk * a
    bonus = _head_sum(r * k * rk_ref[...], ones_bd) * v

    cum3 = _dot(tri_ref[...], jnp.concatenate(_split3(lw), axis=1))
    cum = cum3[:, :MIX_W] + cum3[:, MIX_W:2 * MIX_W] + cum3[:, 2 * MIX_W:]
    ends = [cum[(ch + 1) * L - 1:(ch + 1) * L, :] for ch in range(ncb)]
    cl = jnp.concatenate([jnp.broadcast_to(e, (L, MIX_W)) for e in ends], axis=0) if ncb > 1 else ends[0]
    inv_p = jnp.exp(-cum)
    to_end = jnp.exp(cl - cum)
    at_f = a_s * jnp.exp(cum - lw)
    at = at_f.astype(BF16)
    bt = (b_s * inv_p).astype(BF16)
    kt = (k * inv_p).astype(BF16)
    rt = (r * jnp.exp(cum)).astype(BF16)
    bh = (b_s * to_end).astype(BF16)
    kh = (k * to_end).astype(BF16)
    p_end = [jnp.exp(e) for e in ends]
    vb = v.astype(BF16)

    ri = lax.broadcasted_iota(jnp.int32, (L, 2 * L), 0)
    ci = lax.broadcasted_iota(jnp.int32, (L, 2 * L), 1)
    ci = jnp.where(ci >= L, ci - L, ci)
    strict = ci < ri
    incl = ci <= ri

    chains = [(ch, h) for ch in range(ncb) for h in range(HEADS)]
    sl = lambda arr, ch, h: arr[ch * L:(ch + 1) * L, h * HEAD:(h + 1) * HEAD]
    gm = [_dot_nt(jnp.concatenate([sl(at, ch, h), sl(rt, ch, h)], axis=0),
                  jnp.concatenate([sl(kt, ch, h), sl(bt, ch, h)], axis=0)) for ch, h in chains]
    a_top = [jnp.where(strict, m[:L], 0.0) for m in gm]
    m_low = [jnp.where(incl, m[L:], 0.0).astype(BF16) for m in gm]
    q = [t[:, L:] for t in a_top]
    x = [jnp.concatenate([t[:, :L], sl(at_f, ch, h)], axis=1) for t, (ch, h) in zip(a_top, chains)]
    n_stage = L.bit_length() - 1
    for i in range(n_stage):
        last = i == n_stage - 1
        qb = [t.astype(BF16) for t in q]
        rhs = [t.astype(BF16) for t in x] if last else \
              [jnp.concatenate([t.astype(BF16), u], axis=1) for t, u in zip(x, qb)]
        res = [_dot(u, t) for u, t in zip(qb, rhs)]
        x = [t + u[:, :L + HEAD] for t, u in zip(x, res)]
        if not last:
            q = [u[:, L + HEAD:] for u in res]
    u_free = [_dot(t[:, :L].astype(BF16), sl(vb, ch, h)) for t, (ch, h) in zip(x, chains)]
    wm_rt = [jnp.concatenate([t[:, L:].astype(BF16), sl(rt, ch, h)], axis=0) for t, (ch, h) in zip(x, chains)]
    kb = [jnp.concatenate([sl(kh, ch, h), sl(bh, ch, h)], axis=0) for ch, h in chains]

    state = [s_ref[h] for h in range(HEADS)]
    for ch in range(ncb):
        ids = [ch * HEADS + h for h in range(HEADS)]
        ws = [_dot_nt(wm_rt[i], state[h].astype(BF16)) for h, i in enumerate(ids)]
        uv = [jnp.concatenate([sl(vb, ch, h), (ws[h][:L] + u_free[i]).astype(BF16)], axis=0)
              for h, i in enumerate(ids)]
        for h, i in enumerate(ids):
            y_ref[ch * L:(ch + 1) * L, h * HEAD:(h + 1) * HEAD] = ws[h][L:] + _dot(m_low[i], uv[h])
        state = [state[h] * p_end[ch][:, h * HEAD:(h + 1) * HEAD] + _dot_tn(uv[h], kb[i])
                 for h, i in enumerate(ids)]
    for h in range(HEADS):
        s_ref[h] = state[h]

    y = y_ref[...]
    mu_y = _head_sum(y, ones_bd) * (1.0 / HEAD)
    yc = y - mu_y
    var_y = _head_sum(yc * yc, ones_bd) * (1.0 / HEAD)
    yn = yc * lax.rsqrt(var_y + GN_EPS) * gnw_ref[...] + gnb_ref[...]
    o_ref[...] = ((yn + bonus) * g).astype(BF16)

    @pl.when(c == nc - 1)
    def _():
        sout_ref[...] = s_ref[...]
        shift_out_ref[...] = p[rows - 1:rows, :]


def _tail_plan(n_main, rows, tail):
    if tail is None:
        return n_main, (lambda i: i), [], []
    assert tail.shape == (rows, MIX_W), (tail.shape, rows)
    return (n_main + 1, (lambda i: jnp.minimum(i, n_main - 1)),
            [pl.BlockSpec((rows, MIX_W), lambda i: (0, 0))], [tail])


def _rwkv(proj, tail, shift0, s0, lp, layer, *, row0, batch, seq, chunk, rows):
    nc = seq // rows
    rb0 = row0 // rows
    n_main = batch * nc
    n_steps, clamp, tail_specs, tail_args = _tail_plan(n_main, rows, tail)
    seq_of = lambda i: clamp(i) // nc
    vec = lambda name: pl.BlockSpec((None, 1, lp[name].shape[-1]), lambda i: (layer, 0, 0))
    mat = lambda name: pl.BlockSpec((None,) + lp[name].shape[1:], lambda i: (layer, 0, 0))
    const = lambda arr: pl.BlockSpec(arr.shape, lambda i: (0,) * arr.ndim)
    tri = jnp.kron(jnp.eye(rows // chunk, dtype=F32), jnp.tril(jnp.ones((chunk, chunk), F32))).astype(BF16)
    ones_bd = lp["ones_bd"]
    in_specs = [
        pl.BlockSpec((rows, RWKV_PAD_COLS), lambda i: (rb0 + clamp(i), 0)),
        pl.BlockSpec((None, 1, RWKV_PAD_COLS), lambda i: (seq_of(i), 0, 0)),
        pl.BlockSpec((None, HEADS, HEAD, HEAD), lambda i: (seq_of(i), 0, 0, 0)),
        vec("rwkv_mu"), vec("rwkv_w0"), mat("rwkv_w2"), vec("rwkv_a0"), mat("rwkv_a2"), mat("rwkv_g2"),
        vec("rwkv_kk"), vec("rwkv_ka"), vec("rwkv_rk"), vec("rwkv_gn_w"), vec("rwkv_gn_b"),
        const(ones_bd), const(tri),
    ] + tail_specs
    args = [proj, shift0, s0, lp["rwkv_mu"], lp["rwkv_w0"], lp["rwkv_w2"], lp["rwkv_a0"], lp["rwkv_a2"],
            lp["rwkv_g2"], lp["rwkv_kk"], lp["rwkv_ka"], lp["rwkv_rk"], lp["rwkv_gn_w"], lp["rwkv_gn_b"],
            ones_bd, tri] + tail_args
    return pl.pallas_call(
        functools.partial(_rwkv_kernel, nc=nc, L=chunk, n_main=n_main, has_tail=tail is not None),
        grid=(n_steps,),
        in_specs=in_specs,
        out_specs=[pl.BlockSpec((rows, MIX_W), lambda i: (i, 0)),
                   pl.BlockSpec((None, HEADS, HEAD, HEAD), lambda i: (seq_of(i), 0, 0, 0)),
                   pl.BlockSpec((None, 1, RWKV_PAD_COLS), lambda i: (seq_of(i), 0, 0))],
        out_shape=[jax.ShapeDtypeStruct((n_steps * rows, MIX_W), BF16),
                   jax.ShapeDtypeStruct((batch, HEADS, HEAD, HEAD), F32),
                   jax.ShapeDtypeStruct((batch, 1, RWKV_PAD_COLS), F32)],
        scratch_shapes=[pltpu.VMEM((1, RWKV_PAD_COLS), F32), pltpu.VMEM((HEADS, HEAD, HEAD), F32),
                        pltpu.VMEM((rows, MIX_W), F32)],
        compiler_params=_params(("arbitrary",)),
        name="rwkv",
    )(*args)


def _gmlp_kernel(p_ref, lnw_ref, lnb_ref, ws_ref, bias_ref, *rest, emit_v, n_main, has_tail):
    tail_ref = rest[0] if has_tail else None
    rest = rest[1:] if has_tail else rest
    main = functools.partial(_gmlp_step, p_ref, lnw_ref, lnb_ref, ws_ref, bias_ref, *rest, emit_v=emit_v)
    _with_tail(main, rest[0], tail_ref, n_main)


def _gmlp_step(p_ref, lnw_ref, lnb_ref, ws_ref, bias_ref, o_ref, *rest, emit_v):
    if emit_v:
        v_ref, wm_ref, step = rest
    else:
        wm_ref, step = rest
    L = p_ref.shape[0]

    @pl.when(step == 0)
    def _():
        lower = lax.broadcasted_iota(jnp.int32, (L, L), 1) <= lax.broadcasted_iota(jnp.int32, (L, L), 0)
        for g in range(HEADS):
            wm_ref[g] = jnp.where(lower, ws_ref[g, :L, :L], 0.0).astype(BF16)

    p = p_ref[...]
    u = _gelu_tanh(p[:, :MIX_W])
    v = _layer_norm(_gelu_tanh(p[:, MIX_W:]), lnw_ref[...], lnb_ref[...])
    if emit_v:
        v_ref[...] = v
    vb = v.astype(BF16)
    pair_w = 2 * HEAD
    first = lax.broadcasted_iota(jnp.int32, (L, pair_w), 1) < HEAD
    pairs = [vb[:, pr * pair_w:(pr + 1) * pair_w] for pr in range(HEADS // 2)]
    mixed = [_dot(wm_ref[g], pairs[g // 2]) for g in range(HEADS)]
    mixed = jnp.concatenate([jnp.where(first, mixed[2 * pr], mixed[2 * pr + 1]) for pr in range(HEADS // 2)], axis=1)
    o_ref[...] = (u * (mixed + bias_ref[...])).astype(BF16)


def _gmlp(proj, tail, lp, layer, *, row0, batch, seq, chunk, emit_v):
    rb0 = row0 // chunk
    n_main = batch * seq // chunk
    n_steps, clamp, tail_specs, tail_args = _tail_plan(n_main, chunk, tail)
    bias = lp["gmlp_bias"][:, :chunk, :]
    in_specs = [
        pl.BlockSpec((chunk, 2 * MIX_W), lambda i: (rb0 + clamp(i), COL_GMLP)),
        pl.BlockSpec((None, 1, MIX_W), lambda i: (layer, 0, 0)),
        pl.BlockSpec((None, 1, MIX_W), lambda i: (layer, 0, 0)),
        pl.BlockSpec((None, HEADS, GMLP_CHUNK, GMLP_CHUNK), lambda i: (layer, 0, 0, 0)),
        pl.BlockSpec((None, chunk, MIX_W), lambda i: (layer, 0, 0)),
    ] + tail_specs
    args = [proj, lp["gmlp_ln_w"], lp["gmlp_ln_b"], lp["gmlp_ws"], bias] + tail_args
    out_specs = [pl.BlockSpec((chunk, MIX_W), lambda i: (i, 0))]
    out_shape = [jax.ShapeDtypeStruct((n_steps * chunk, MIX_W), BF16)]
    if emit_v:
        out_specs.append(pl.BlockSpec((chunk, MIX_W), lambda i: (clamp(i), 0)))
        out_shape.append(jax.ShapeDtypeStruct((n_main * chunk, MIX_W), F32))
    return pl.pallas_call(
        functools.partial(_gmlp_kernel, emit_v=emit_v, n_main=n_main, has_tail=tail is not None),
        grid=(n_steps,),
        in_specs=in_specs,
        out_specs=out_specs,
        out_shape=out_shape,
        scratch_shapes=[pltpu.VMEM((HEADS, chunk, chunk), BF16)],
        compiler_params=_params(("arbitrary",)),
        name="gmlp",
    )(*args)


def _conv_kernel(p_ref, prev_ref, dw_ref, dwb_ref, lnw_ref, lnb_ref, *rest, nt, n_main, has_tail):
    tail_ref = rest[0] if has_tail else None
    rest = rest[1:] if has_tail else rest
    main = functools.partial(_conv_step, p_ref, prev_ref, dw_ref, dwb_ref, lnw_ref, lnb_ref, *rest, nt=nt)
    _with_tail(main, rest[0], tail_ref, n_main)


def _conv_step(p_ref, prev_ref, dw_ref, dwb_ref, lnw_ref, lnb_ref, o_ref, state_ref, z_ref, step, *, nt):
    tb = p_ref.shape[0]
    t = lax.rem(step, nt)
    off = CONV_TAIL - (CONV_W - 1)

    @pl.when(t == 0)
    def _():
        z_ref[0, 0:CONV_TAIL, :] = prev_ref[...]

    @pl.when(t > 0)
    def _():
        z_ref[0, 0:CONV_TAIL, :] = z_ref[0, tb:tb + CONV_TAIL, :]

    p = p_ref[...]
    z_ref[0, CONV_TAIL:CONV_TAIL + tb, :] = p[:, :MIX_W] * jax.nn.sigmoid(p[:, MIX_W:])
    span = tb + CONV_TAIL - SUBLANES
    for s in range(1, SUBLANES):
        z_ref[s, 0:span, :] = z_ref[0, s:s + span, :]
    sub = min(tb, 32)
    for r0 in range(0, tb, sub):
        acc = None
        for w in range(CONV_W):
            a, s = divmod(w + off, SUBLANES)
            term = z_ref[s, r0 + a * SUBLANES:r0 + a * SUBLANES + sub, :] * dw_ref[w:w + 1, :]
            acc = term if acc is None else acc + term
        y = _layer_norm(acc + dwb_ref[...], lnw_ref[...], lnb_ref[...])
        o_ref[r0:r0 + sub, :] = jax.nn.silu(y).astype(BF16)

    @pl.when(t == nt - 1)
    def _():
        state_ref[...] = z_ref[0, tb + off:tb + CONV_TAIL, :]


def _conv(proj, tail, prev, lp, layer, *, row0, batch, seq, tb):
    nt = seq // tb
    rb0 = row0 // tb
    n_main = batch * nt
    n_steps, clamp, tail_specs, tail_args = _tail_plan(n_main, tb, tail)
    seq_of = lambda i: clamp(i) // nt
    in_specs = [
        pl.BlockSpec((tb, 2 * MIX_W), lambda i: (rb0 + clamp(i), COL_CONV)),
        pl.BlockSpec((None, CONV_TAIL, MIX_W), lambda i: (seq_of(i), 0, 0)),
        pl.BlockSpec((None, CONV_W, MIX_W), lambda i: (layer, 0, 0)),
        pl.BlockSpec((None, 1, MIX_W), lambda i: (layer, 0, 0)),
        pl.BlockSpec((None, 1, MIX_W), lambda i: (layer, 0, 0)),
        pl.BlockSpec((None, 1, MIX_W), lambda i: (layer, 0, 0)),
    ] + tail_specs
    args = [proj, prev, lp["conv_dw"], lp["conv_dw_b"], lp["conv_ln_w"], lp["conv_ln_b"]] + tail_args
    return pl.pallas_call(
        functools.partial(_conv_kernel, nt=nt, n_main=n_main, has_tail=tail is not None),
        grid=(n_steps,),
        in_specs=in_specs,
        out_specs=[pl.BlockSpec((tb, MIX_W), lambda i: (i, 0)),
                   pl.BlockSpec((None, CONV_W - 1, MIX_W), lambda i: (seq_of(i), 0, 0))],
        out_shape=[jax.ShapeDtypeStruct((n_steps * tb, MIX_W), BF16),
                   jax.ShapeDtypeStruct((batch, CONV_W - 1, MIX_W), F32)],
        scratch_shapes=[pltpu.VMEM((SUBLANES, tb + CONV_TAIL, MIX_W), F32)],
        compiler_params=_params(("arbitrary",)),
        name="conv",
    )(*args)


def _rel_bias_kernel(table_ref, o_ref):
    width = BAND + CHUNK
    u = lax.broadcasted_iota(jnp.int32, (REL_PAD, width), 1)
    m = lax.broadcasted_iota(jnp.int32, (REL_PAD, width), 0)
    idx = jnp.clip(BAND - 1 - u, -REL_CLIP, REL_CLIP) + REL_CLIP
    onehot = (idx == m).astype(F32)
    ext = jnp.dot(table_ref[...], onehot, preferred_element_type=F32, precision=lax.Precision.HIGHEST)
    for i in range(CHUNK):
        o_ref[i] = ext[:, CHUNK - 1 - i:CHUNK - 1 - i + BAND]


def _rel_bias(table_pad):
    depth = table_pad.shape[0]
    return pl.pallas_call(
        _rel_bias_kernel,
        grid=(depth,),
        in_specs=[pl.BlockSpec((None, HEADS, REL_PAD), lambda l: (l, 0, 0))],
        out_specs=pl.BlockSpec((None, CHUNK, HEADS, BAND), lambda l: (l, 0, 0, 0)),
        out_shape=jax.ShapeDtypeStruct((depth, CHUNK, HEADS, BAND), F32),
        compiler_params=_params(("parallel",)),
        name="rel_bias",
    )(table_pad)


def _attend_heads(q, k, v, bias_of, valid):
    assert HEAD ** -0.5 == 2.0 ** -3
    q = (q * (HEAD ** -0.5)).astype(BF16)
    rq = q.shape[0]
    pair_w = 2 * HEAD
    first = lax.broadcasted_iota(jnp.int32, (rq, pair_w), 1) < HEAD
    zero = jnp.zeros((rq, pair_w), q.dtype)
    units = []
    for pr in range(HEADS // 2):
        ps = slice(pr * pair_w, (pr + 1) * pair_w)
        for half in range(2):
            units.append((2 * pr + half, jnp.where(first if half == 0 else ~first, q[:, ps], zero), k[:, ps], v[:, ps]))
    s = [_dot_nt(qm, k2) + bias_of(h) for h, qm, k2, _ in units]
    if valid is not None:
        s = [jnp.where(valid, t, -1e30) for t in s]
    e = [jnp.exp(t - jnp.max(t, axis=-1, keepdims=True)) for t in s]
    pr_ = [(t * (1.0 / jnp.sum(t, axis=-1, keepdims=True))).astype(BF16) for t in e]
    o = [_dot(t, v2) for t, (_, _, _, v2) in zip(pr_, units)]
    return jnp.concatenate([jnp.where(first, o[2 * pr], o[2 * pr + 1]) for pr in range(HEADS // 2)], axis=1)


def _attn_prompt_kernel(q_ref, k_ref, v_ref, bias_ref, *rest, nq, n_main, has_tail):
    tail_ref = rest[0] if has_tail else None
    rest = rest[1:] if has_tail else rest
    main = functools.partial(_attn_prompt_step, q_ref, k_ref, v_ref, bias_ref, *rest, nq=nq)
    _with_tail(main, rest[0], tail_ref, n_main)


def _attn_prompt_step(q_ref, k_ref, v_ref, bias_ref, o_ref, knew_ref, vnew_ref, kp_ref, vp_ref, step, *, nq):
    rows = q_ref.shape[0]
    seq = k_ref.shape[0]
    keep = knew_ref.shape[0]
    c = lax.rem(step, nq)

    @pl.when(c == 0)
    def _():
        kp_ref[0:BAND_PAST, :] = jnp.zeros((BAND_PAST, MIX_W), BF16)
        vp_ref[0:BAND_PAST, :] = jnp.zeros((BAND_PAST, MIX_W), BF16)
        kp_ref[BAND_PAST:, :] = k_ref[...].astype(BF16)
        vp_ref[BAND_PAST:, :] = v_ref[...].astype(BF16)
        knew_ref[...] = k_ref[seq - keep:, :]
        vnew_ref[...] = v_ref[seq - keep:, :]

    def attend(masked):
        col = lax.broadcasted_iota(jnp.int32, (CHUNK, BAND), 1)
        outs = []
        for cb in range(rows // CHUNK):
            r0 = pl.multiple_of(c * rows + cb * CHUNK, CHUNK)
            outs.append(_attend_heads(q_ref[cb * CHUNK:(cb + 1) * CHUNK, :], kp_ref[pl.ds(r0, BAND), :],
                                      vp_ref[pl.ds(r0, BAND), :], lambda h: bias_ref[h],
                                      (r0 + col) >= BAND_PAST if masked else None))
        o_ref[...] = jnp.concatenate(outs, axis=0).astype(BF16)

    reaches_start = c * rows < BAND_PAST
    pl.when(reaches_start)(lambda: attend(True))
    pl.when(jnp.logical_not(reaches_start))(lambda: attend(False))


def _attn_prompt(proj, tail, bias, layer, *, batch, seq, keep, rows):
    nq = seq // rows
    n_main = batch * nq
    n_steps, clamp, tail_specs, tail_args = _tail_plan(n_main, rows, tail)
    seq_of = lambda i: clamp(i) // nq
    return pl.pallas_call(
        functools.partial(_attn_prompt_kernel, nq=nq, n_main=n_main, has_tail=tail is not None),
        grid=(n_steps,),
        in_specs=[
            pl.BlockSpec((rows, MIX_W), lambda i: (clamp(i), 0)),
            pl.BlockSpec((seq, MIX_W), lambda i: (seq_of(i), 1)),
            pl.BlockSpec((seq, MIX_W), lambda i: (seq_of(i), 2)),
            pl.BlockSpec((None, HEADS, CHUNK, BAND), lambda i: (layer, 0, 0, 0)),
        ] + tail_specs,
        out_specs=[pl.BlockSpec((rows, MIX_W), lambda i: (i, 0)),
                   pl.BlockSpec((None, keep, MIX_W), lambda i: (seq_of(i), 0, 0)),
                   pl.BlockSpec((None, keep, MIX_W), lambda i: (seq_of(i), 0, 0))],
        out_shape=[jax.ShapeDtypeStruct((n_steps * rows, MIX_W), BF16),
                   jax.ShapeDtypeStruct((batch, keep, MIX_W), F32),
                   jax.ShapeDtypeStruct((batch, keep, MIX_W), F32)],
        scratch_shapes=[pltpu.VMEM((BAND_PAST + seq, MIX_W), BF16), pltpu.VMEM((BAND_PAST + seq, MIX_W), BF16)],
        compiler_params=_params(("arbitrary",)),
        name="attn_prompt",
    )(proj, proj, proj, bias, *tail_args)


def _attn_sample_kernel(q_ref, k_ref, v_ref, ck_ref, cv_ref, bias_ref, o_ref, knew_ref, vnew_ref, kp_ref, vp_ref):
    t = q_ref.shape[0]
    past = ck_ref.shape[0]
    kp_ref[0:past, :] = ck_ref[...].astype(BF16)
    vp_ref[0:past, :] = cv_ref[...].astype(BF16)
    kp_ref[past:, :] = k_ref[...].astype(BF16)
    vp_ref[past:, :] = v_ref[...].astype(BF16)
    knew_ref[...] = k_ref[...]
    vnew_ref[...] = v_ref[...]
    o = _attend_heads(q_ref[...], kp_ref[...], vp_ref[...], lambda h: bias_ref[h, :t, :past + t], None)
    o_ref[...] = o.astype(BF16)


def _attn_sample(proj, cache_k, cache_v, bias, layer, *, row0, batch, seq):
    rb0 = row0 // seq
    past = cache_k.shape[2]
    return pl.pallas_call(
        _attn_sample_kernel,
        grid=(batch,),
        in_specs=[
            pl.BlockSpec((seq, MIX_W), lambda b: (rb0 + b, 0)),
            pl.BlockSpec((seq, MIX_W), lambda b: (rb0 + b, 1)),
            pl.BlockSpec((seq, MIX_W), lambda b: (rb0 + b, 2)),
            pl.BlockSpec((None, None, past, MIX_W), lambda b: (layer, b, 0, 0)),
            pl.BlockSpec((None, None, past, MIX_W), lambda b: (layer, b, 0, 0)),
            pl.BlockSpec((None, HEADS, CHUNK, BAND), lambda b: (layer, 0, 0, 0)),
        ],
        out_specs=[pl.BlockSpec((seq, MIX_W), lambda b: (b, 0)),
                   pl.BlockSpec((None, seq, MIX_W), lambda b: (b, 0, 0)),
                   pl.BlockSpec((None, seq, MIX_W), lambda b: (b, 0, 0))],
        out_shape=[jax.ShapeDtypeStruct((batch * seq, MIX_W), BF16),
                   jax.ShapeDtypeStruct((batch, seq, MIX_W), F32),
                   jax.ShapeDtypeStruct((batch, seq, MIX_W), F32)],
        scratch_shapes=[pltpu.VMEM((past + seq, MIX_W), BF16), pltpu.VMEM((past + seq, MIX_W), BF16)],
        compiler_params=_params(("parallel",)),
        name="attn_sample",
    )(proj, proj, proj, cache_k, cache_v, bias)


def _pad_rwkv_cols(x):
    return jnp.pad(x, [(0, 0)] * (x.ndim - 1) + [(0, RWKV_PAD_COLS - RWKV_COLS)])


def _lora_rows(w, start, first, last):
    return jnp.pad(w, ((0, 0), (start - first, last - start - w.shape[1]), (0, 0))).astype(BF16)


def kernel(x_prompt, x_sample, state_rwkv_shift, state_rwkv_wkv, cache_conv, cache_attn_k, cache_attn_v,
           norm_mix, norm_ffn, norm_final, w_in, rwkv_mu, rwkv_w0, rwkv_w2, rwkv_a0, rwkv_a2, rwkv_g2,
           rwkv_kk, rwkv_ka, rwkv_rk, rwkv_gn_w, rwkv_gn_b, gmlp_ln_w, gmlp_ln_b, gmlp_ws, gmlp_bs,
           conv_dw, conv_dw_b, conv_ln_w, conv_ln_b, attn_rel_bias, w_branch, w_out, w_ffn_in, w_ffn_out):
    bp, seq, d = x_prompt.shape
    bs, dseq, _ = x_sample.shape
    depth = w_in.shape[0]
    rows_p = bp * seq
    rows_s = bs * dseq
    assert seq % GMLP_CHUNK == 0 and dseq <= CHUNK and rows_p % dseq == 0

    w_mix = w_gate = jnp.swapaxes(w_in, 1, 2).astype(BF16)
    w_branch_b = w_branch.astype(BF16)
    w_out_b = w_out.astype(BF16)
    w_ffn_in_b = w_ffn_in.astype(BF16)
    w_ffn_out_b = w_ffn_out.astype(BF16)
    row3 = lambda p: p.reshape(depth, 1, -1)
    lane_tile = 128
    lp = {
        "rwkv_mu": row3(_pad_rwkv_cols(rwkv_mu)), "rwkv_w0": row3(rwkv_w0), "rwkv_a0": row3(rwkv_a0),
        "rwkv_w2": _lora_rows(rwkv_w2, 0, 0, lane_tile),
        "rwkv_a2": _lora_rows(rwkv_a2, W_RANK, 0, 2 * lane_tile),
        "rwkv_g2": _lora_rows(rwkv_g2, W_RANK + A_RANK, lane_tile, MIX_W),
        "rwkv_kk": row3(rwkv_kk), "rwkv_ka": row3(rwkv_ka),
        "rwkv_rk": row3(rwkv_rk), "rwkv_gn_w": row3(rwkv_gn_w), "rwkv_gn_b": row3(rwkv_gn_b),
        "ones_bd": jnp.kron(jnp.eye(MXU_TILE // HEAD, dtype=F32), jnp.ones((HEAD, HEAD), F32)).astype(BF16),
        "gmlp_ln_w": row3(gmlp_ln_w), "gmlp_ln_b": row3(gmlp_ln_b), "gmlp_ws": gmlp_ws,
        "gmlp_bias": jnp.repeat(jnp.swapaxes(gmlp_bs, 1, 2), HEAD, axis=2),
        "conv_dw": conv_dw, "conv_dw_b": row3(conv_dw_b), "conv_ln_w": row3(conv_ln_w),
        "conv_ln_b": row3(conv_ln_b),
    }
    g_mix = row3(norm_mix)
    g_ffn = row3(norm_ffn)
    table_pad = jnp.pad(attn_rel_bias, ((0, 0), (0, 0), (0, REL_PAD - attn_rel_bias.shape[-1])))
    bias = jnp.swapaxes(_rel_bias(table_pad), 1, 2)

    past = cache_attn_k.shape[2]
    cache_k = cache_attn_k.reshape(depth, bs, past, MIX_W)
    cache_v = cache_attn_v.reshape(depth, bs, past, MIX_W)
    shift_s = _pad_rwkv_cols(state_rwkv_shift).reshape(depth, bs, 1, RWKV_PAD_COLS)
    conv_s = jnp.pad(cache_conv, ((0, 0), (0, 0), (CONV_TAIL - (CONV_W - 1), 0), (0, 0)))
    shift_p = jnp.zeros((bp, 1, RWKV_PAD_COLS), F32)
    wkv_p = jnp.zeros((bp, HEADS, HEAD, HEAD), F32)
    conv_p = jnp.zeros((bp, CONV_TAIL, MIX_W), F32)

    x = jnp.concatenate([x_prompt.reshape(rows_p, d), x_sample.reshape(rows_s, d)], axis=0)
    keep = min(BAND_PAST, seq)
    assert rows_s == MIXER_STEP_ROWS and seq % MIXER_STEP_ROWS == 0
    outs = {k: [] for k in ("p_shift", "p_wkv", "p_conv", "p_k", "p_v",
                            "s_shift", "s_wkv", "s_conv", "s_k", "s_v", "s_gv")}
    for l in range(depth):
        proj, h = _inproj(x, g_mix, w_mix, l)
        qkv = _attn_proj(h, w_mix, l)

        s_rwkv, wkv_new_s, shift_new_s = _rwkv(proj, None, shift_s[l], state_rwkv_wkv[l], lp, l, row0=rows_p,
                                               batch=bs, seq=dseq, chunk=dseq, rows=dseq)
        o_rwkv, wkv_new_p, shift_new_p = _rwkv(proj, s_rwkv, shift_p, wkv_p, lp, l, row0=0, batch=bp, seq=seq,
                                               chunk=CHUNK, rows=MIXER_STEP_ROWS)
        s_gmlp, gv_s = _gmlp(proj, None, lp, l, row0=rows_p, batch=bs, seq=dseq, chunk=dseq, emit_v=True)
        (o_gmlp,) = _gmlp(proj, s_gmlp, lp, l, row0=0, batch=bp, seq=seq, chunk=GMLP_CHUNK, emit_v=False)
        s_conv, conv_new_s = _conv(proj, None, conv_s[l], lp, l, row0=rows_p, batch=bs, seq=dseq, tb=dseq)
        o_conv, conv_new_p = _conv(proj, s_conv, conv_p, lp, l, row0=0, batch=bp, seq=seq, tb=MIXER_STEP_ROWS)
        s_attn, k_new_s, v_new_s = _attn_sample(qkv, cache_k, cache_v, bias, l, row0=rows_p, batch=bs, seq=dseq)
        o_attn, k_new_p, v_new_p = _attn_prompt(qkv, s_attn, bias, l, batch=bp, seq=seq, keep=keep,
                                                rows=MIXER_STEP_ROWS)

        merged = _merge(h, w_gate, (o_rwkv, o_gmlp, o_conv, o_attn), w_branch_b, l)
        x = _resid(merged, w_out_b, x, l)
        act = _ffn_in(x, g_ffn, w_ffn_in_b, l)
        x = _resid(act, w_ffn_out_b, x, l)

        outs["p_shift"].append(shift_new_p[:, 0, :RWKV_COLS])
        outs["p_wkv"].append(wkv_new_p)
        outs["p_conv"].append(conv_new_p)
        outs["p_k"].append(k_new_p.reshape(bp, keep, HEADS, HEAD))
        outs["p_v"].append(v_new_p.reshape(bp, keep, HEADS, HEAD))
        outs["s_shift"].append(shift_new_s[:, 0, :RWKV_COLS])
        outs["s_wkv"].append(wkv_new_s)
        outs["s_conv"].append(conv_new_s)
        outs["s_k"].append(k_new_s.reshape(bs, dseq, HEADS, HEAD))
        outs["s_v"].append(v_new_s.reshape(bs, dseq, HEADS, HEAD))
        outs["s_gv"].append(gv_s.reshape(bs, dseq, MIX_W))

    g_final = norm_final.reshape(1, d)
    y_p = _final_norm(x, g_final, row0=0, rows=rows_p)
    y_s = _final_norm(x, g_final, row0=rows_p, rows=rows_s)
    st = {k: jnp.stack(v) for k, v in outs.items()}
    return (y_p.reshape(bp, seq, d), y_s.reshape(bs, dseq, d),
            st["p_shift"], st["p_wkv"], st["p_conv"], st["p_k"], st["p_v"],
            st["s_shift"], st["s_wkv"], st["s_conv"], st["s_k"], st["s_v"], st["s_gv"])
```

```python
import functools

import jax
import jax.numpy as jnp
from jax import lax
from jax.experimental import pallas as pl
from jax.experimental.pallas import tpu as pltpu

F32 = jnp.float32
BF16 = jnp.bfloat16

MIX_W = 512
N_BRANCH = 4
HEADS = 8
HEAD = 64
W_RANK = 96
A_RANK = 96
G_RANK = 256
RWKV_COLS = 3 * MIX_W + W_RANK + A_RANK + G_RANK
RWKV_PAD_COLS = 4 * MIX_W
GATE_ROW0 = RWKV_COLS + 7 * MIX_W
GN_EPS = 64e-5
RMS_EPS = 1e-6
LN_EPS = 1e-5
CHUNK = 64
GMLP_CHUNK = 128
CONV_W = 31
SUBLANES = 8
CONV_TAIL = 32
BAND_PAST = 8 * CHUNK
BAND = BAND_PAST + CHUNK
REL_CLIP = 128
REL_PAD = 384
SEQ_COLS = RWKV_PAD_COLS + 2 * MIX_W + 2 * MIX_W
ATTN_COLS = 3 * MIX_W
ATTN_ROW0 = RWKV_COLS + 4 * MIX_W
COL_GMLP = RWKV_PAD_COLS // (2 * MIX_W)
COL_CONV = COL_GMLP + 1

VMEM_LIMIT_BYTES = 52 * 1024 * 1024
VMEM_LIMIT_MERGE_BYTES = 58 * 1024 * 1024
ROW_TILE_SMALL = 768
ROW_TILE_LARGE = 1408
COL_TILE_TARGET = 512
MIXER_STEP_ROWS = 2 * CHUNK
MXU_TILE = 256


def _pick_tile(n, target, mult):
    best = None
    for t in range(mult, min(n, target) + 1, mult):
        if n % t == 0:
            best = t
    assert best is not None, (n, target, mult)
    return best


def _params(sem, vmem_limit_bytes=VMEM_LIMIT_BYTES):
    return pltpu.CompilerParams(dimension_semantics=sem, vmem_limit_bytes=vmem_limit_bytes)


def _rms_bf16(x, g):
    ms = jnp.mean(x * x, axis=-1, keepdims=True)
    return (x * lax.rsqrt(ms + RMS_EPS) * g).astype(BF16)


def _layer_norm(x, w, b):
    mu = jnp.mean(x, axis=-1, keepdims=True)
    var = jnp.mean(jnp.square(x - mu), axis=-1, keepdims=True)
    return (x - mu) * lax.rsqrt(var + LN_EPS) * w + b


def _gelu_tanh(x):
    return 0.5 * x * (1.0 + jnp.tanh(0.7978845608028654 * (x + 0.044715 * x * x * x)))


def _dot(a, b):
    return jnp.dot(a, b, preferred_element_type=F32)


def _dot_nt(a, b):
    return lax.dot_general(a, b, (((1,), (1,)), ((), ())), preferred_element_type=F32)


def _dot_tn(a, b):
    return lax.dot_general(a, b, (((0,), (0,)), ((), ())), preferred_element_type=F32)


def _inproj_kernel(x_ref, g_ref, w_ref, proj_ref, h_ref):
    @pl.when(pl.program_id(1) == 0)
    def _():
        h_ref[...] = _rms_bf16(x_ref[...], g_ref[...])

    proj_ref[...] = _dot_nt(h_ref[...], w_ref[0])


def _attn_proj_kernel(h_ref, w_ref, o_ref):
    o_ref[...] = _dot_nt(h_ref[...], w_ref[0])


def _attn_proj(h, w_in_t, layer):
    m, d = h.shape
    tm = _pick_tile(m, ROW_TILE_LARGE, 16)
    return pl.pallas_call(
        _attn_proj_kernel,
        grid=(m // tm,),
        in_specs=[
            pl.BlockSpec((tm, d), lambda i: (i, 0)),
            pl.BlockSpec((pl.Element(1), pl.Element(ATTN_COLS), pl.Element(d)), lambda i: (layer, ATTN_ROW0, 0)),
        ],
        out_specs=pl.BlockSpec((tm, ATTN_COLS), lambda i: (i, 0)),
        out_shape=jax.ShapeDtypeStruct((m, ATTN_COLS), F32),
        compiler_params=_params(("parallel",)),
        name="attn_proj",
    )(h, w_in_t)


def _inproj(x, g, w_in_t, layer):
    m, d = x.shape
    n = SEQ_COLS
    tm = _pick_tile(m, ROW_TILE_SMALL, 16)
    tn = _pick_tile(n, RWKV_PAD_COLS, 128)
    assert RWKV_PAD_COLS % tn == 0
    n_rwkv = RWKV_PAD_COLS // tn
    overlap = RWKV_PAD_COLS - RWKV_COLS
    row_start = lambda j: pl.multiple_of(j * tn - jnp.where(j >= n_rwkv, overlap, 0), overlap)
    return pl.pallas_call(
        _inproj_kernel,
        grid=(m // tm, n // tn),
        in_specs=[
            pl.BlockSpec((tm, d), lambda i, j: (i, 0)),
            pl.BlockSpec((None, 1, d), lambda i, j: (layer, 0, 0)),
            pl.BlockSpec((pl.Element(1), pl.Element(tn), pl.Element(d)), lambda i, j: (layer, row_start(j), 0)),
        ],
        out_specs=[
            pl.BlockSpec((tm, tn), lambda i, j: (i, j)),
            pl.BlockSpec((tm, d), lambda i, j: (i, 0)),
        ],
        out_shape=[jax.ShapeDtypeStruct((m, n), F32), jax.ShapeDtypeStruct((m, d), BF16)],
        compiler_params=_params(("parallel", "arbitrary")),
        name="inproj",
    )(x, g, w_in_t)


def _merge_kernel(h_ref, wg0, wg1, wg2, wg3, b0, b1, b2, b3, wb_ref, o_ref):
    h = h_ref[...]
    tn = o_ref.shape[1]
    half = tn // 2
    for c0 in (0, half):
        acc = None
        for n, (wg, br) in enumerate(((wg0, b0), (wg1, b1), (wg2, b2), (wg3, b3))):
            gate = jax.nn.sigmoid(_dot_nt(h, wg[0, c0:c0 + half, :]))
            term = gate * _dot(br[...], wb_ref[n, :, c0:c0 + half])
            acc = term if acc is None else acc + term
        o_ref[:, c0:c0 + half] = acc.astype(BF16)


def _merge(h, w_gate, branches, w_branch, layer):
    m, d = h.shape
    tm = _pick_tile(m, ROW_TILE_LARGE, 16)
    tn = _pick_tile(d, COL_TILE_TARGET, 128)
    nj = d // tn
    gate_specs = [pl.BlockSpec((pl.Element(1), pl.Element(tn), pl.Element(d)),
                               functools.partial(lambda i, j, n: (layer, pl.multiple_of(GATE_ROW0 + n * d + j * tn, 64), 0),
                                                 n=n))
                  for n in range(N_BRANCH)]
    br_specs = [pl.BlockSpec((tm, MIX_W), lambda i, j: (i, 0)) for _ in range(N_BRANCH)]
    return pl.pallas_call(
        _merge_kernel,
        grid=(m // tm, nj),
        in_specs=[pl.BlockSpec((tm, d), lambda i, j: (i, 0))] + gate_specs + br_specs
        + [pl.BlockSpec((None, N_BRANCH, MIX_W, tn), lambda i, j: (layer, 0, 0, j))],
        out_specs=pl.BlockSpec((tm, tn), lambda i, j: (i, j)),
        out_shape=jax.ShapeDtypeStruct((m, d), BF16),
        compiler_params=_params(("parallel", "arbitrary"), VMEM_LIMIT_MERGE_BYTES),
        name="merge",
    )(h, w_gate, w_gate, w_gate, w_gate, *branches, w_branch)


def _resid_kernel(a_ref, w_ref, x_ref, o_ref):
    o_ref[...] = x_ref[...] + _dot(a_ref[...], w_ref[...])


def _resid(a, w, x, layer):
    m, k = a.shape
    d = x.shape[-1]
    tm = _pick_tile(m, ROW_TILE_LARGE if 8 * ROW_TILE_LARGE * k * 2 <= VMEM_LIMIT_BYTES else ROW_TILE_SMALL, 16)
    blocks = lambda tn: 2 * (tm * k * 2 + k * tn * 2 + 2 * tm * tn * 4)
    tn = max(t for t in (_pick_tile(d, COL_TILE_TARGET, 128), _pick_tile(d, 2 * COL_TILE_TARGET, 128))
             if t <= COL_TILE_TARGET or blocks(t) <= VMEM_LIMIT_BYTES)
    return pl.pallas_call(
        _resid_kernel,
        grid=(m // tm, d // tn),
        in_specs=[
            pl.BlockSpec((tm, k), lambda i, j: (i, 0)),
            pl.BlockSpec((None, k, tn), lambda i, j: (layer, 0, j)),
            pl.BlockSpec((tm, tn), lambda i, j: (i, j)),
        ],
        out_specs=pl.BlockSpec((tm, tn), lambda i, j: (i, j)),
        out_shape=jax.ShapeDtypeStruct((m, d), F32),
        compiler_params=_params(("parallel", "arbitrary")),
        name="resid",
    )(a, w, x)


def _ffn_in_kernel(x_ref, g_ref, wg_ref, wu_ref, act_ref, h_scr):
    @pl.when(pl.program_id(1) == 0)
    def _():
        h_scr[...] = _rms_bf16(x_ref[...], g_ref[...])

    h = h_scr[...]
    act_ref[...] = (jax.nn.silu(_dot(h, wg_ref[...])) * _dot(h, wu_ref[...])).astype(BF16)


def _ffn_in(x, g, w_ffn_in, layer):
    m, d = x.shape
    dff = w_ffn_in.shape[-1] // 2
    tm = _pick_tile(m, ROW_TILE_LARGE, 16)
    tn = _pick_tile(dff, COL_TILE_TARGET, 128)
    nj = dff // tn
    return pl.pallas_call(
        _ffn_in_kernel,
        grid=(m // tm, nj),
        in_specs=[
            pl.BlockSpec((tm, d), lambda i, j: (i, 0)),
            pl.BlockSpec((None, 1, d), lambda i, j: (layer, 0, 0)),
            pl.BlockSpec((None, d, tn), lambda i, j: (layer, 0, j)),
            pl.BlockSpec((None, d, tn), lambda i, j: (layer, 0, nj + j)),
        ],
        out_specs=pl.BlockSpec((tm, tn), lambda i, j: (i, j)),
        out_shape=jax.ShapeDtypeStruct((m, dff), BF16),
        scratch_shapes=[pltpu.VMEM((tm, d), BF16)],
        compiler_params=_params(("parallel", "arbitrary")),
        name="ffn_in",
    )(x, g, w_ffn_in, w_ffn_in)


def _final_norm_kernel(x_ref, g_ref, o_ref):
    x = x_ref[...]
    ms = jnp.mean(x * x, axis=-1, keepdims=True)
    o_ref[...] = x * lax.rsqrt(ms + RMS_EPS) * g_ref[...]


def _final_norm(x, g, *, row0, rows):
    d = x.shape[-1]
    tm = _pick_tile(rows, ROW_TILE_SMALL, 16)
    assert row0 % tm == 0
    rb0 = row0 // tm
    return pl.pallas_call(
        _final_norm_kernel,
        grid=(rows // tm,),
        in_specs=[pl.BlockSpec((tm, d), lambda i: (rb0 + i, 0)), pl.BlockSpec((1, d), lambda i: (0, 0))],
        out_specs=pl.BlockSpec((tm, d), lambda i: (i, 0)),
        out_shape=jax.ShapeDtypeStruct((rows, d), F32),
        compiler_params=_params(("parallel",)),
        name="final_norm",
    )(x, g)


def _split3(x):
    hi = x.astype(BF16)
    r1 = x - hi.astype(F32)
    mid = r1.astype(BF16)
    lo = (r1 - mid.astype(F32)).astype(BF16)
    return hi, mid, lo


def _head_sum(x, ones_bd):
    rows = x.shape[0]
    w = ones_bd.shape[0]
    hi = x.astype(BF16)
    lo = (x - hi.astype(F32)).astype(BF16)
    st = jnp.concatenate([hi, lo], axis=0)
    s = jnp.concatenate([_dot(st[:, j:j + w], ones_bd) for j in range(0, x.shape[1], w)], axis=1)
    return s[:rows] + s[rows:]


def _with_tail(main, o_ref, tail_ref, n_main):
    i = pl.program_id(0)
    if tail_ref is None:
        main(i)
        return

    @pl.when(i < n_main)
    def _():
        main(i)

    @pl.when(i == n_main)
    def _():
        o_ref[...] = tail_ref[...]


def _rwkv_kernel(p_ref, shift0_ref, s0_ref, mu_ref, w0_ref, w2_ref, a0_ref, a2_ref, g2_ref, kkw_ref, ka_ref,
                 rk_ref, gnw_ref, gnb_ref, ones_ref, tri_ref, *rest, nc, L, n_main, has_tail):
    tail_ref = rest[0] if has_tail else None
    o_ref, sout_ref, shift_out_ref, carry_ref, s_ref, y_ref = rest[1:] if has_tail else rest
    main = functools.partial(_rwkv_step, p_ref, shift0_ref, s0_ref, mu_ref, w0_ref, w2_ref, a0_ref, a2_ref, g2_ref,
                             kkw_ref, ka_ref, rk_ref, gnw_ref, gnb_ref, ones_ref, tri_ref, o_ref, sout_ref,
                             shift_out_ref, carry_ref, s_ref, y_ref, nc=nc, L=L)
    _with_tail(main, o_ref, tail_ref, n_main)


def _rwkv_step(p_ref, shift0_ref, s0_ref, mu_ref, w0_ref, w2_ref, a0_ref, a2_ref, g2_ref, kkw_ref, ka_ref,
               rk_ref, gnw_ref, gnb_ref, ones_ref, tri_ref, o_ref, sout_ref, shift_out_ref,
               carry_ref, s_ref, y_ref, step, *, nc, L):
    rows = p_ref.shape[0]
    ncb = rows // L
    c = lax.rem(step, nc)

    @pl.when(c == 0)
    def _():
        carry_ref[...] = shift0_ref[...]
        s_ref[...] = s0_ref[...]

    p = p_ref[...]
    row = lax.broadcasted_iota(jnp.int32, p.shape, 0)
    prev = jnp.where(row == 0, carry_ref[...], pltpu.roll(p, 1, 0))
    carry_ref[...] = p[rows - 1:rows, :]
    xs = p + mu_ref[...] * (prev - p)

    o_w = 3 * MIX_W
    r = xs[:, 0:MIX_W]
    k = xs[:, MIX_W:2 * MIX_W]
    v = xs[:, 2 * MIX_W:3 * MIX_W]
    wl = xs[:, o_w:o_w + w2_ref.shape[0]]
    al = xs[:, o_w:o_w + a2_ref.shape[0]]
    gl = xs[:, o_w + MIX_W - g2_ref.shape[0]:o_w + MIX_W]
    ones_bd = ones_ref[...]

    zz = w0_ref[...] + _dot(jnp.tanh(wl).astype(BF16), w2_ref[...])
    nz = -zz
    softplus = jnp.maximum(nz, 0.0) + jnp.log(1.0 + jnp.exp(-jnp.abs(nz)))
    lw = -jnp.exp(-softplus - 0.5)
    a = jax.nn.sigmoid(a0_ref[...] + _dot(al.astype(BF16), a2_ref[...]))
    g = _dot(jax.nn.sigmoid(gl).astype(BF16), g2_ref[...])
    kk = k * kkw_ref[...]
    kk = kk * lax.rsqrt(jnp.maximum(_head_sum(kk * kk, ones_bd), 1e-24))
    k = k * (1.0 + (a - 1.0) * ka_ref[...])
    a_s = -kk
    b_s = kk * a
    bonus = _head_sum(r * k * rk_ref[...], ones_bd) * v

    cum3 = _dot(tri_ref[...], jnp.concatenate(_split3(lw), axis=1))
    cum = cum3[:, :MIX_W] + cum3[:, MIX_W:2 * MIX_W] + cum3[:, 2 * MIX_W:]
    ends = [cum[(ch + 1) * L - 1:(ch + 1) * L, :] for ch in range(ncb)]
    cl = jnp.concatenate([jnp.broadcast_to(e, (L, MIX_W)) for e in ends], axis=0) if ncb > 1 else ends[0]
    inv_p = jnp.exp(-cum)
    to_end = jnp.exp(cl - cum)
    at_f = a_s * jnp.exp(cum - lw)
    at = at_f.astype(BF16)
    bt = (b_s * inv_p).astype(BF16)
    kt = (k * inv_p).astype(BF16)
    rt = (r * jnp.exp(cum)).astype(BF16)
    bh = (b_s * to_end).astype(BF16)
    kh = (k * to_end).astype(BF16)
    p_end = [jnp.exp(e) for e in ends]
    vb = v.astype(BF16)

    ri = lax.broadcasted_iota(jnp.int32, (L, 2 * L), 0)
    ci = lax.broadcasted_iota(jnp.int32, (L, 2 * L), 1)
    ci = jnp.where(ci >= L, ci - L, ci)
    strict = ci < ri
    incl = ci <= ri

    chains = [(ch, h) for ch in range(ncb) for h in range(HEADS)]
    sl = lambda arr, ch, h: arr[ch * L:(ch + 1) * L, h * HEAD:(h + 1) * HEAD]
    gm = [_dot_nt(jnp.concatenate([sl(at, ch, h), sl(rt, ch, h)], axis=0),
                  jnp.concatenate([sl(kt, ch, h), sl(bt, ch, h)], axis=0)) for ch, h in chains]
    a_top = [jnp.where(strict, m[:L], 0.0) for m in gm]
    m_low = [jnp.where(incl, m[L:], 0.0).astype(BF16) for m in gm]
    q = [t[:, L:] for t in a_top]
    x = [jnp.concatenate([t[:, :L], sl(at_f, ch, h)], axis=1) for t, (ch, h) in zip(a_top, chains)]
    n_stage = L.bit_length() - 1
    for i in range(n_stage):
        last = i == n_stage - 1
        qb = [t.astype(BF16) for t in q]
        rhs = [t.astype(BF16) for t in x] if last else \
              [jnp.concatenate([t.astype(BF16), u], axis=1) for t, u in zip(x, qb)]
        res = [_dot(u, t) for u, t in zip(qb, rhs)]
        x = [t + u[:, :L + HEAD] for t, u in zip(x, res)]
        if not last:
            q = [u[:, L + HEAD:] for u in res]
    u_free = [_dot(t[:, :L].astype(BF16), sl(vb, ch, h)) for t, (ch, h) in zip(x, chains)]
    wm_rt = [jnp.concatenate([t[:, L:].astype(BF16), sl(rt, ch, h)], axis=0) for t, (ch, h) in zip(x, chains)]
    kb = [jnp.concatenate([sl(kh, ch, h), sl(bh, ch, h)], axis=0) for ch, h in chains]

    state = [s_ref[h] for h in range(HEADS)]
    for ch in range(ncb):
        ids = [ch * HEADS + h for h in range(HEADS)]
        ws = [_dot_nt(wm_rt[i], state[h].astype(BF16)) for h, i in enumerate(ids)]
        uv = [jnp.concatenate([sl(vb, ch, h), (ws[h][:L] + u_free[i]).astype(BF16)], axis=0)
              for h, i in enumerate(ids)]
        for h, i in enumerate(ids):
            y_ref[ch * L:(ch + 1) * L, h * HEAD:(h + 1) * HEAD] = ws[h][L:] + _dot(m_low[i], uv[h])
        state = [state[h] * p_end[ch][:, h * HEAD:(h + 1) * HEAD] + _dot_tn(uv[h], kb[i])
                 for h, i in enumerate(ids)]
    for h in range(HEADS):
        s_ref[h] = state[h]

    y = y_ref[...]
    mu_y = _head_sum(y, ones_bd) * (1.0 / HEAD)
    yc = y - mu_y
    var_y = _head_sum(yc * yc, ones_bd) * (1.0 / HEAD)
    yn = yc * lax.rsqrt(var_y + GN_EPS) * gnw_ref[...] + gnb_ref[...]
    o_ref[...] = ((yn + bonus) * g).astype(BF16)

    @pl.when(c == nc - 1)
    def _():
        sout_ref[...] = s_ref[...]
        shift_out_ref[...] = p[rows - 1:rows, :]


def _tail_plan(n_main, rows, tail):
    if tail is None:
        return n_main, (lambda i: i), [], []
    assert tail.shape == (rows, MIX_W), (tail.shape, rows)
    return (n_main + 1, (lambda i: jnp.minimum(i, n_main - 1)),
            [pl.BlockSpec((rows, MIX_W), lambda i: (0, 0))], [tail])


def _rwkv(proj, tail, shift0, s0, lp, layer, *, row0, batch, seq, chunk, rows):
    nc = seq // rows
    rb0 = row0 // rows
    n_main = batch * nc
    n_steps, clamp, tail_specs, tail_args = _tail_plan(n_main, rows, tail)
    seq_of = lambda i: clamp(i) // nc
    vec = lambda name: pl.BlockSpec((None, 1, lp[name].shape[-1]), lambda i: (layer, 0, 0))
    mat = lambda name: pl.BlockSpec((None,) + lp[name].shape[1:], lambda i: (layer, 0, 0))
    const = lambda arr: pl.BlockSpec(arr.shape, lambda i: (0,) * arr.ndim)
    tri = jnp.kron(jnp.eye(rows // chunk, dtype=F32), jnp.tril(jnp.ones((chunk, chunk), F32))).astype(BF16)
    ones_bd = lp["ones_bd"]
    in_specs = [
        pl.BlockSpec((rows, RWKV_PAD_COLS), lambda i: (rb0 + clamp(i), 0)),
        pl.BlockSpec((None, 1, RWKV_PAD_COLS), lambda i: (seq_of(i), 0, 0)),
        pl.BlockSpec((None, HEADS, HEAD, HEAD), lambda i: (seq_of(i), 0, 0, 0)),
        vec("rwkv_mu"), vec("rwkv_w0"), mat("rwkv_w2"), vec("rwkv_a0"), mat("rwkv_a2"), mat("rwkv_g2"),
        vec("rwkv_kk"), vec("rwkv_ka"), vec("rwkv_rk"), vec("rwkv_gn_w"), vec("rwkv_gn_b"),
        const(ones_bd), const(tri),
    ] + tail_specs
    args = [proj, shift0, s0, lp["rwkv_mu"], lp["rwkv_w0"], lp["rwkv_w2"], lp["rwkv_a0"], lp["rwkv_a2"],
            lp["rwkv_g2"], lp["rwkv_kk"], lp["rwkv_ka"], lp["rwkv_rk"], lp["rwkv_gn_w"], lp["rwkv_gn_b"],
            ones_bd, tri] + tail_args
    return pl.pallas_call(
        functools.partial(_rwkv_kernel, nc=nc, L=chunk, n_main=n_main, has_tail=tail is not None),
        grid=(n_steps,),
        in_specs=in_specs,
        out_specs=[pl.BlockSpec((rows, MIX_W), lambda i: (i, 0)),
                   pl.BlockSpec((None, HEADS, HEAD, HEAD), lambda i: (seq_of(i), 0, 0, 0)),
                   pl.BlockSpec((None, 1, RWKV_PAD_COLS), lambda i: (seq_of(i), 0, 0))],
        out_shape=[jax.ShapeDtypeStruct((n_steps * rows, MIX_W), BF16),
                   jax.ShapeDtypeStruct((batch, HEADS, HEAD, HEAD), F32),
                   jax.ShapeDtypeStruct((batch, 1, RWKV_PAD_COLS), F32)],
        scratch_shapes=[pltpu.VMEM((1, RWKV_PAD_COLS), F32), pltpu.VMEM((HEADS, HEAD, HEAD), F32),
                        pltpu.VMEM((rows, MIX_W), F32)],
        compiler_params=_params(("arbitrary",)),
        name="rwkv",
    )(*args)


def _gmlp_kernel(p_ref, lnw_ref, lnb_ref, ws_ref, bias_ref, *rest, emit_v, n_main, has_tail):
    tail_ref = rest[0] if has_tail else None
    rest = rest[1:] if has_tail else rest
    main = functools.partial(_gmlp_step, p_ref, lnw_ref, lnb_ref, ws_ref, bias_ref, *rest, emit_v=emit_v)
    _with_tail(main, rest[0], tail_ref, n_main)


def _gmlp_step(p_ref, lnw_ref, lnb_ref, ws_ref, bias_ref, o_ref, *rest, emit_v):
    if emit_v:
        v_ref, wm_ref, step = rest
    else:
        wm_ref, step = rest
    L = p_ref.shape[0]

    @pl.when(step == 0)
    def _():
        lower = lax.broadcasted_iota(jnp.int32, (L, L), 1) <= lax.broadcasted_iota(jnp.int32, (L, L), 0)
        for g in range(HEADS):
            wm_ref[g] = jnp.where(lower, ws_ref[g, :L, :L], 0.0).astype(BF16)

    p = p_ref[...]
    u = _gelu_tanh(p[:, :MIX_W])
    v = _layer_norm(_gelu_tanh(p[:, MIX_W:]), lnw_ref[...], lnb_ref[...])
    if emit_v:
        v_ref[...] = v
    vb = v.astype(BF16)
    pair_w = 2 * HEAD
    first = lax.broadcasted_iota(jnp.int32, (L, pair_w), 1) < HEAD
    pairs = [vb[:, pr * pair_w:(pr + 1) * pair_w] for pr in range(HEADS // 2)]
    mixed = [_dot(wm_ref[g], pairs[g // 2]) for g in range(HEADS)]
    mixed = jnp.concatenate([jnp.where(first, mixed[2 * pr], mixed[2 * pr + 1]) for pr in range(HEADS // 2)], axis=1)
    o_ref[...] = (u * (mixed + bias_ref[...])).astype(BF16)


def _gmlp(proj, tail, lp, layer, *, row0, batch, seq, chunk, emit_v):
    rb0 = row0 // chunk
    n_main = batch * seq // chunk
    n_steps, clamp, tail_specs, tail_args = _tail_plan(n_main, chunk, tail)
    bias = lp["gmlp_bias"][:, :chunk, :]
    in_specs = [
        pl.BlockSpec((chunk, 2 * MIX_W), lambda i: (rb0 + clamp(i), COL_GMLP)),
        pl.BlockSpec((None, 1, MIX_W), lambda i: (layer, 0, 0)),
        pl.BlockSpec((None, 1, MIX_W), lambda i: (layer, 0, 0)),
        pl.BlockSpec((None, HEADS, GMLP_CHUNK, GMLP_CHUNK), lambda i: (layer, 0, 0, 0)),
        pl.BlockSpec((None, chunk, MIX_W), lambda i: (layer, 0, 0)),
    ] + tail_specs
    args = [proj, lp["gmlp_ln_w"], lp["gmlp_ln_b"], lp["gmlp_ws"], bias] + tail_args
    out_specs = [pl.BlockSpec((chunk, MIX_W), lambda i: (i, 0))]
    out_shape = [jax.ShapeDtypeStruct((n_steps * chunk, MIX_W), BF16)]
    if emit_v:
        out_specs.append(pl.BlockSpec((chunk, MIX_W), lambda i: (clamp(i), 0)))
        out_shape.append(jax.ShapeDtypeStruct((n_main * chunk, MIX_W), F32))
    return pl.pallas_call(
        functools.partial(_gmlp_kernel, emit_v=emit_v, n_main=n_main, has_tail=tail is not None),
        grid=(n_steps,),
        in_specs=in_specs,
        out_specs=out_specs,
        out_shape=out_shape,
        scratch_shapes=[pltpu.VMEM((HEADS, chunk, chunk), BF16)],
        compiler_params=_params(("arbitrary",)),
        name="gmlp",
    )(*args)


def _conv_kernel(p_ref, prev_ref, dw_ref, dwb_ref, lnw_ref, lnb_ref, *rest, nt, n_main, has_tail):
    tail_ref = rest[0] if has_tail else None
    rest = rest[1:] if has_tail else rest
    main = functools.partial(_conv_step, p_ref, prev_ref, dw_ref, dwb_ref, lnw_ref, lnb_ref, *rest, nt=nt)
    _with_tail(main, rest[0], tail_ref, n_main)


def _conv_step(p_ref, prev_ref, dw_ref, dwb_ref, lnw_ref, lnb_ref, o_ref, state_ref, z_ref, step, *, nt):
    tb = p_ref.shape[0]
    t = lax.rem(step, nt)
    off = CONV_TAIL - (CONV_W - 1)

    @pl.when(t == 0)
    def _():
        z_ref[0, 0:CONV_TAIL, :] = prev_ref[...]

    @pl.when(t > 0)
    def _():
        z_ref[0, 0:CONV_TAIL, :] = z_ref[0, tb:tb + CONV_TAIL, :]

    p = p_ref[...]
    z_ref[0, CONV_TAIL:CONV_TAIL + tb, :] = p[:, :MIX_W] * jax.nn.sigmoid(p[:, MIX_W:])
    span = tb + CONV_TAIL - SUBLANES
    for s in range(1, SUBLANES):
        z_ref[s, 0:span, :] = z_ref[0, s:s + span, :]
    sub = min(tb, 32)
    for r0 in range(0, tb, sub):
        acc = None
        for w in range(CONV_W):
            a, s = divmod(w + off, SUBLANES)
            term = z_ref[s, r0 + a * SUBLANES:r0 + a * SUBLANES + sub, :] * dw_ref[w:w + 1, :]
            acc = term if acc is None else acc + term
        y = _layer_norm(acc + dwb_ref[...], lnw_ref[...], lnb_ref[...])
        o_ref[r0:r0 + sub, :] = jax.nn.silu(y).astype(BF16)

    @pl.when(t == nt - 1)
    def _():
        state_ref[...] = z_ref[0, tb + off:tb + CONV_TAIL, :]


def _conv(proj, tail, prev, lp, layer, *, row0, batch, seq, tb):
    nt = seq // tb
    rb0 = row0 // tb
    n_main = batch * nt
    n_steps, clamp, tail_specs, tail_args = _tail_plan(n_main, tb, tail)
    seq_of = lambda i: clamp(i) // nt
    in_specs = [
        pl.BlockSpec((tb, 2 * MIX_W), lambda i: (rb0 + clamp(i), COL_CONV)),
        pl.BlockSpec((None, CONV_TAIL, MIX_W), lambda i: (seq_of(i), 0, 0)),
        pl.BlockSpec((None, CONV_W, MIX_W), lambda i: (layer, 0, 0)),
        pl.BlockSpec((None, 1, MIX_W), lambda i: (layer, 0, 0)),
        pl.BlockSpec((None, 1, MIX_W), lambda i: (layer, 0, 0)),
        pl.BlockSpec((None, 1, MIX_W), lambda i: (layer, 0, 0)),
    ] + tail_specs
    args = [proj, prev, lp["conv_dw"], lp["conv_dw_b"], lp["conv_ln_w"], lp["conv_ln_b"]] + tail_args
    return pl.pallas_call(
        functools.partial(_conv_kernel, nt=nt, n_main=n_main, has_tail=tail is not None),
        grid=(n_steps,),
        in_specs=in_specs,
        out_specs=[pl.BlockSpec((tb, MIX_W), lambda i: (i, 0)),
                   pl.BlockSpec((None, CONV_W - 1, MIX_W), lambda i: (seq_of(i), 0, 0))],
        out_shape=[jax.ShapeDtypeStruct((n_steps * tb, MIX_W), BF16),
                   jax.ShapeDtypeStruct((batch, CONV_W - 1, MIX_W), F32)],
        scratch_shapes=[pltpu.VMEM((SUBLANES, tb + CONV_TAIL, MIX_W), F32)],
        compiler_params=_params(("arbitrary",)),
        name="conv",
    )(*args)


def _rel_bias_kernel(table_ref, o_ref):
    width = BAND + CHUNK
    u = lax.broadcasted_iota(jnp.int32, (REL_PAD, width), 1)
    m = lax.broadcasted_iota(jnp.int32, (REL_PAD, width), 0)
    idx = jnp.clip(BAND - 1 - u, -REL_CLIP, REL_CLIP) + REL_CLIP
    onehot = (idx == m).astype(F32)
    ext = jnp.dot(table_ref[...], onehot, preferred_element_type=F32, precision=lax.Precision.HIGHEST)
    for i in range(CHUNK):
        o_ref[i] = ext[:, CHUNK - 1 - i:CHUNK - 1 - i + BAND]


def _rel_bias(table_pad):
    depth = table_pad.shape[0]
    return pl.pallas_call(
        _rel_bias_kernel,
        grid=(depth,),
        in_specs=[pl.BlockSpec((None, HEADS, REL_PAD), lambda l: (l, 0, 0))],
        out_specs=pl.BlockSpec((None, CHUNK, HEADS, BAND), lambda l: (l, 0, 0, 0)),
        out_shape=jax.ShapeDtypeStruct((depth, CHUNK, HEADS, BAND), F32),
        compiler_params=_params(("parallel",)),
        name="rel_bias",
    )(table_pad)


def _attend_heads(q, k, v, bias_of, valid):
    assert HEAD ** -0.5 == 2.0 ** -3
    q = (q * (HEAD ** -0.5)).astype(BF16)
    rq = q.shape[0]
    pair_w = 2 * HEAD
    first = lax.broadcasted_iota(jnp.int32, (rq, pair_w), 1) < HEAD
    zero = jnp.zeros((rq, pair_w), q.dtype)
    units = []
    for pr in range(HEADS // 2):
        ps = slice(pr * pair_w, (pr + 1) * pair_w)
        for half in range(2):
            units.append((2 * pr + half, jnp.where(first if half == 0 else ~first, q[:, ps], zero), k[:, ps], v[:, ps]))
    s = [_dot_nt(qm, k2) + bias_of(h) for h, qm, k2, _ in units]
    if valid is not None:
        s = [jnp.where(valid, t, -1e30) for t in s]
    e = [jnp.exp(t - jnp.max(t, axis=-1, keepdims=True)) for t in s]
    pr_ = [(t * (1.0 / jnp.sum(t, axis=-1, keepdims=True))).astype(BF16) for t in e]
    o = [_dot(t, v2) for t, (_, _, _, v2) in zip(pr_, units)]
    return jnp.concatenate([jnp.where(first, o[2 * pr], o[2 * pr + 1]) for pr in range(HEADS // 2)], axis=1)


def _attn_prompt_kernel(q_ref, k_ref, v_ref, bias_ref, *rest, nq, n_main, has_tail):
    tail_ref = rest[0] if has_tail else None
    rest = rest[1:] if has_tail else rest
    main = functools.partial(_attn_prompt_step, q_ref, k_ref, v_ref, bias_ref, *rest, nq=nq)
    _with_tail(main, rest[0], tail_ref, n_main)


def _attn_prompt_step(q_ref, k_ref, v_ref, bias_ref, o_ref, knew_ref, vnew_ref, kp_ref, vp_ref, step, *, nq):
    rows = q_ref.shape[0]
    seq = k_ref.shape[0]
    keep = knew_ref.shape[0]
    c = lax.rem(step, nq)

    @pl.when(c == 0)
    def _():
        kp_ref[0:BAND_PAST, :] = jnp.zeros((BAND_PAST, MIX_W), BF16)
        vp_ref[0:BAND_PAST, :] = jnp.zeros((BAND_PAST, MIX_W), BF16)
        kp_ref[BAND_PAST:, :] = k_ref[...].astype(BF16)
        vp_ref[BAND_PAST:, :] = v_ref[...].astype(BF16)
        knew_ref[...] = k_ref[seq - keep:, :]
        vnew_ref[...] = v_ref[seq - keep:, :]

    def attend(masked):
        col = lax.broadcasted_iota(jnp.int32, (CHUNK, BAND), 1)
        outs = []
        for cb in range(rows // CHUNK):
            r0 = pl.multiple_of(c * rows + cb * CHUNK, CHUNK)
            outs.append(_attend_heads(q_ref[cb * CHUNK:(cb + 1) * CHUNK, :], kp_ref[pl.ds(r0, BAND), :],
                                      vp_ref[pl.ds(r0, BAND), :], lambda h: bias_ref[h],
                                      (r0 + col) >= BAND_PAST if masked else None))
        o_ref[...] = jnp.concatenate(outs, axis=0).astype(BF16)

    reaches_start = c * rows < BAND_PAST
    pl.when(reaches_start)(lambda: attend(True))
    pl.when(jnp.logical_not(reaches_start))(lambda: attend(False))


def _attn_prompt(proj, tail, bias, layer, *, batch, seq, keep, rows):
    nq = seq // rows
    n_main = batch * nq
    n_steps, clamp, tail_specs, tail_args = _tail_plan(n_main, rows, tail)
    seq_of = lambda i: clamp(i) // nq
    return pl.pallas_call(
        functools.partial(_attn_prompt_kernel, nq=nq, n_main=n_main, has_tail=tail is not None),
        grid=(n_steps,),
        in_specs=[
            pl.BlockSpec((rows, MIX_W), lambda i: (clamp(i), 0)),
            pl.BlockSpec((seq, MIX_W), lambda i: (seq_of(i), 1)),
            pl.BlockSpec((seq, MIX_W), lambda i: (seq_of(i), 2)),
            pl.BlockSpec((None, HEADS, CHUNK, BAND), lambda i: (layer, 0, 0, 0)),
        ] + tail_specs,
        out_specs=[pl.BlockSpec((rows, MIX_W), lambda i: (i, 0)),
                   pl.BlockSpec((None, keep, MIX_W), lambda i: (seq_of(i), 0, 0)),
                   pl.BlockSpec((None, keep, MIX_W), lambda i: (seq_of(i), 0, 0))],
        out_shape=[jax.ShapeDtypeStruct((n_steps * rows, MIX_W), BF16),
                   jax.ShapeDtypeStruct((batch, keep, MIX_W), F32),
                   jax.ShapeDtypeStruct((batch, keep, MIX_W), F32)],
        scratch_shapes=[pltpu.VMEM((BAND_PAST + seq, MIX_W), BF16), pltpu.VMEM((BAND_PAST + seq, MIX_W), BF16)],
        compiler_params=_params(("arbitrary",)),
        name="attn_prompt",
    )(proj, proj, proj, bias, *tail_args)


def _attn_sample_kernel(q_ref, k_ref, v_ref, ck_ref, cv_ref, bias_ref, o_ref, knew_ref, vnew_ref, kp_ref, vp_ref):
    t = q_ref.shape[0]
    past = ck_ref.shape[0]
    kp_ref[0:past, :] = ck_ref[...].astype(BF16)
    vp_ref[0:past, :] = cv_ref[...].astype(BF16)
    kp_ref[past:, :] = k_ref[...].astype(BF16)
    vp_ref[past:, :] = v_ref[...].astype(BF16)
    knew_ref[...] = k_ref[...]
    vnew_ref[...] = v_ref[...]
    o = _attend_heads(q_ref[...], kp_ref[...], vp_ref[...], lambda h: bias_ref[h, :t, :past + t], None)
    o_ref[...] = o.astype(BF16)


def _attn_sample(proj, cache_k, cache_v, bias, layer, *, row0, batch, seq):
    rb0 = row0 // seq
    past = cache_k.shape[2]
    return pl.pallas_call(
        _attn_sample_kernel,
        grid=(batch,),
        in_specs=[
            pl.BlockSpec((seq, MIX_W), lambda b: (rb0 + b, 0)),
            pl.BlockSpec((seq, MIX_W), lambda b: (rb0 + b, 1)),
            pl.BlockSpec((seq, MIX_W), lambda b: (rb0 + b, 2)),
            pl.BlockSpec((None, None, past, MIX_W), lambda b: (layer, b, 0, 0)),
            pl.BlockSpec((None, None, past, MIX_W), lambda b: (layer, b, 0, 0)),
            pl.BlockSpec((None, HEADS, CHUNK, BAND), lambda b: (layer, 0, 0, 0)),
        ],
        out_specs=[pl.BlockSpec((seq, MIX_W), lambda b: (b, 0)),
                   pl.BlockSpec((None, seq, MIX_W), lambda b: (b, 0, 0)),
                   pl.BlockSpec((None, seq, MIX_W), lambda b: (b, 0, 0))],
        out_shape=[jax.ShapeDtypeStruct((batch * seq, MIX_W), BF16),
                   jax.ShapeDtypeStruct((batch, seq, MIX_W), F32),
                   jax.ShapeDtypeStruct((batch, seq, MIX_W), F32)],
        scratch_shapes=[pltpu.VMEM((past + seq, MIX_W), BF16), pltpu.VMEM((past + seq, MIX_W), BF16)],
        compiler_params=_params(("parallel",)),
        name="attn_sample",
    )(proj, proj, proj, cache_k, cache_v, bias)


def _pad_rwkv_cols(x):
    return jnp.pad(x, [(0, 0)] * (x.ndim - 1) + [(0, RWKV_PAD_COLS - RWKV_COLS)])


def _lora_rows(w, start, first, last):
    return jnp.pad(w, ((0, 0), (start - first, last - start - w.shape[1]), (0, 0))).astype(BF16)


def kernel(x_prompt, x_sample, state_rwkv_shift, state_rwkv_wkv, cache_conv, cache_attn_k, cache_attn_v,
           norm_mix, norm_ffn, norm_final, w_in, rwkv_mu, rwkv_w0, rwkv_w2, rwkv_a0, rwkv_a2, rwkv_g2,
           rwkv_kk, rwkv_ka, rwkv_rk, rwkv_gn_w, rwkv_gn_b, gmlp_ln_w, gmlp_ln_b, gmlp_ws, gmlp_bs,
           conv_dw, conv_dw_b, conv_ln_w, conv_ln_b, attn_rel_bias, w_branch, w_out, w_ffn_in, w_ffn_out):
    bp, seq, d = x_prompt.shape
    bs, dseq, _ = x_sample.shape
    depth = w_in.shape[0]
    rows_p = bp * seq
    rows_s = bs * dseq
    assert seq % GMLP_CHUNK == 0 and dseq <= CHUNK and rows_p % dseq == 0

    w_mix = w_gate = jnp.swapaxes(w_in, 1, 2).astype(BF16)
    w_branch_b = w_branch.astype(BF16)
    w_out_b = w_out.astype(BF16)
    w_ffn_in_b = w_ffn_in.astype(BF16)
    w_ffn_out_b = w_ffn_out.astype(BF16)
    row3 = lambda p: p.reshape(depth, 1, -1)
    lane_tile = 128
    lp = {
        "rwkv_mu": row3(_pad_rwkv_cols(rwkv_mu)), "rwkv_w0": row3(rwkv_w0), "rwkv_a0": row3(rwkv_a0),
        "rwkv_w2": _lora_rows(rwkv_w2, 0, 0, lane_tile),
        "rwkv_a2": _lora_rows(rwkv_a2, W_RANK, 0, 2 * lane_tile),
        "rwkv_g2": _lora_rows(rwkv_g2, W_RANK + A_RANK, lane_tile, MIX_W),
        "rwkv_kk": row3(rwkv_kk), "rwkv_ka": row3(rwkv_ka),
        "rwkv_rk": row3(rwkv_rk), "rwkv_gn_w": row3(rwkv_gn_w), "rwkv_gn_b": row3(rwkv_gn_b),
        "ones_bd": jnp.kron(jnp.eye(MXU_TILE // HEAD, dtype=F32), jnp.ones((HEAD, HEAD), F32)).astype(BF16),
        "gmlp_ln_w": row3(gmlp_ln_w), "gmlp_ln_b": row3(gmlp_ln_b), "gmlp_ws": gmlp_ws,
        "gmlp_bias": jnp.repeat(jnp.swapaxes(gmlp_bs, 1, 2), HEAD, axis=2),
        "conv_dw": conv_dw, "conv_dw_b": row3(conv_dw_b), "conv_ln_w": row3(conv_ln_w),
        "conv_ln_b": row3(conv_ln_b),
    }
    g_mix = row3(norm_mix)
    g_ffn = row3(norm_ffn)
    table_pad = jnp.pad(attn_rel_bias, ((0, 0), (0, 0), (0, REL_PAD - attn_rel_bias.shape[-1])))
    bias = jnp.swapaxes(_rel_bias(table_pad), 1, 2)

    past = cache_attn_k.shape[2]
    cache_k = cache_attn_k.reshape(depth, bs, past, MIX_W)
    cache_v = cache_attn_v.reshape(depth, bs, past, MIX_W)
    shift_s = _pad_rwkv_cols(state_rwkv_shift).reshape(depth, bs, 1, RWKV_PAD_COLS)
    conv_s = jnp.pad(cache_conv, ((0, 0), (0, 0), (CONV_TAIL - (CONV_W - 1), 0), (0, 0)))
    shift_p = jnp.zeros((bp, 1, RWKV_PAD_COLS), F32)
    wkv_p = jnp.zeros((bp, HEADS, HEAD, HEAD), F32)
    conv_p = jnp.zeros((bp, CONV_TAIL, MIX_W), F32)

    x = jnp.concatenate([x_prompt.reshape(rows_p, d), x_sample.reshape(rows_s, d)], axis=0)
    keep = min(BAND_PAST, seq)
    assert rows_s == MIXER_STEP_ROWS and seq % MIXER_STEP_ROWS == 0
    outs = {k: [] for k in ("p_shift", "p_wkv", "p_conv", "p_k", "p_v",
                            "s_shift", "s_wkv", "s_conv", "s_k", "s_v", "s_gv")}
    for l in range(depth):
        proj, h = _inproj(x, g_mix, w_mix, l)
        qkv = _attn_proj(h, w_mix, l)

        s_rwkv, wkv_new_s, shift_new_s = _rwkv(proj, None, shift_s[l], state_rwkv_wkv[l], lp, l, row0=rows_p,
                                               batch=bs, seq=dseq, chunk=dseq, rows=dseq)
        o_rwkv, wkv_new_p, shift_new_p = _rwkv(proj, s_rwkv, shift_p, wkv_p, lp, l, row0=0, batch=bp, seq=seq,
                                               chunk=CHUNK, rows=MIXER_STEP_ROWS)
        s_gmlp, gv_s = _gmlp(proj, None, lp, l, row0=rows_p, batch=bs, seq=dseq, chunk=dseq, emit_v=True)
        (o_gmlp,) = _gmlp(proj, s_gmlp, lp, l, row0=0, batch=bp, seq=seq, chunk=GMLP_CHUNK, emit_v=False)
        s_conv, conv_new_s = _conv(proj, None, conv_s[l], lp, l, row0=rows_p, batch=bs, seq=dseq, tb=dseq)
        o_conv, conv_new_p = _conv(proj, s_conv, conv_p, lp, l, row0=0, batch=bp, seq=seq, tb=MIXER_STEP_ROWS)
        s_attn, k_new_s, v_new_s = _attn_sample(qkv, cache_k, cache_v, bias, l, row0=rows_p, batch=bs, seq=dseq)
        o_attn, k_new_p, v_new_p = _attn_prompt(qkv, s_attn, bias, l, batch=bp, seq=seq, keep=keep,
                                                rows=MIXER_STEP_ROWS)

        merged = _merge(h, w_gate, (o_rwkv, o_gmlp, o_conv, o_attn), w_branch_b, l)
        x = _resid(merged, w_out_b, x, l)
        act = _ffn_in(x, g_ffn, w_ffn_in_b, l)
        x = _resid(act, w_ffn_out_b, x, l)

        outs["p_shift"].append(shift_new_p[:, 0, :RWKV_COLS])
        outs["p_wkv"].append(wkv_new_p)
        outs["p_conv"].append(conv_new_p)
        outs["p_k"].append(k_new_p.reshape(bp, keep, HEADS, HEAD))
        outs["p_v"].append(v_new_p.reshape(bp, keep, HEADS, HEAD))
        outs["s_shift"].append(shift_new_s[:, 0, :RWKV_COLS])
        outs["s_wkv"].append(wkv_new_s)
        outs["s_conv"].append(conv_new_s)
        outs["s_k"].append(k_new_s.reshape(bs, dseq, HEADS, HEAD))
        outs["s_v"].append(v_new_s.reshape(bs, dseq, HEADS, HEAD))
        outs["s_gv"].append(gv_s.reshape(bs, dseq, MIX_W))

    g_final = norm_final.reshape(1, d)
    y_p = _final_norm(x, g_final, row0=0, rows=rows_p)
    y_s = _final_norm(x, g_final, row0=rows_p, rows=rows_s)
    st = {k: jnp.stack(v) for k, v in outs.items()}
    return (y_p.reshape(bp, seq, d), y_s.reshape(bs, dseq, d),
            st["p_shift"], st["p_wkv"], st["p_conv"], st["p_k"], st["p_v"],
            st["s_shift"], st["s_wkv"], st["s_conv"], st["s_k"], st["s_v"], st["s_gv"])
```
